```python
import math
import jax, jax.numpy as jnp
from jax import lax
import numpy as np

D_MODEL = 1024
BATCH = 4
SEQ = 8192
DEPTH = 2

N_MIXERS = 2
N_ATTN_LAYERS = (DEPTH + 1) // 2
N_DELTA_LAYERS = DEPTH // 2
PLE_DIM = 256
EPS = 1e-6

SWA_GROUPS = ((128, 1), (512, 4), (2048, 16))
N_GROUPS = len(SWA_GROUPS)
A_HEADS = 8
A_HEAD_DIM = 64
A_OUT_WIDTH = A_HEADS * A_HEAD_DIM
A_QKV_WIDTH = N_GROUPS * 3 * A_OUT_WIDTH
ROPE_DIM = A_HEAD_DIM // 4
ROPE_THETA = 500000.0
BAND_BLOCK = 128

DN_HEADS = 8
DN_HEAD_DIM = 128
DN_WIDTH = DN_HEADS * DN_HEAD_DIM
DN_IN_WIDTH = 3 * DN_WIDTH + 2 * DN_HEADS + DN_WIDTH
CONV_WIDTH = 4
CHUNK = 64

D_FF = 4 * D_MODEL

kernel_name = 'hybrid_dilated_swa_gated_deltanet'


def rmsnorm(x, gain):
    xf = x.astype(jnp.float32)
    y = xf * lax.rsqrt(jnp.mean(xf * xf, axis=-1, keepdims=True) + EPS)
    return (y * gain.astype(jnp.float32)).astype(x.dtype)


def l2norm(x):
    xf = x.astype(jnp.float32)
    return xf * lax.rsqrt(jnp.sum(xf * xf, axis=-1, keepdims=True) + EPS)


def rope_tables(positions):
    inv_freq = ROPE_THETA ** (-jnp.arange(0, ROPE_DIM, 2, dtype=jnp.float32) / ROPE_DIM)
    ang = positions.astype(jnp.float32)[..., None] * inv_freq
    return jnp.cos(ang)[:, :, None, :], jnp.sin(ang)[:, :, None, :]


def apply_partial_rope(x, cos, sin):
    half = ROPE_DIM // 2
    xf = x.astype(jnp.float32)
    x1, x2 = xf[..., :half], xf[..., half:ROPE_DIM]
    out = jnp.concatenate([x1 * cos - x2 * sin, x2 * cos + x1 * sin, xf[..., ROPE_DIM:]], axis=-1)
    return out.astype(x.dtype)


def dilated_band_attention(q, k, v, window, dilation):
    B, S, H, E = q.shape
    L = S // dilation
    span = window // dilation
    n_prev = -(-span // BAND_BLOCK)
    nblk = -(-L // BAND_BLOCK)
    Lp = nblk * BAND_BLOCK

    def by_residue(t):
        t = t.reshape(B, L, dilation, H, E).transpose(0, 2, 1, 3, 4)
        t = jnp.pad(t, ((0, 0), (0, 0), (0, Lp - L), (0, 0), (0, 0)))
        return t.reshape(B, dilation, nblk, BAND_BLOCK, H, E)

    def band(t):
        tp = jnp.pad(t, ((0, 0), (0, 0), (n_prev, 0), (0, 0), (0, 0), (0, 0)))
        return jnp.concatenate([tp[:, :, s:s + nblk] for s in range(n_prev + 1)], axis=3)

    qb = by_residue(q)
    kw = band(by_residue(k))
    vw = band(by_residue(v))
    qi = jnp.arange(BAND_BLOCK)[:, None]
    kj = jnp.arange((n_prev + 1) * BAND_BLOCK)[None, :]
    dist = qi + n_prev * BAND_BLOCK - kj
    k_abs = (jnp.arange(nblk)[:, None, None] - n_prev) * BAND_BLOCK + kj[None]
    valid = (dist >= 0)[None] & (dist <= span)[None] & (k_abs >= 0)

    s = jnp.einsum('bdnqhe,bdnkhe->bdnhqk', qb, kw).astype(jnp.float32) * (E ** -0.5)
    s = jnp.where(valid[None, None, :, None], s, -jnp.inf)
    m = jnp.max(s, axis=-1, keepdims=True)
    e = jnp.exp(s - m)
    l = jnp.sum(e, axis=-1, keepdims=True)
    o = jnp.einsum('bdnhqk,bdnkhe->bdnqhe', (e / l).astype(v.dtype), vw)
    lse = (m + jnp.log(l))[..., 0]
    o = o.reshape(B, dilation, Lp, H, E)[:, :, :L].transpose(0, 2, 1, 3, 4).reshape(B, S, H, E)
    lse = lse.transpose(0, 1, 2, 4, 3).reshape(B, dilation, Lp, H)[:, :, :L]
    lse = lse.transpose(0, 2, 1, 3).reshape(B, S, H)
    return o, lse


def dilated_attention_mixer(h, cos, sin, w_qkv, q_gain, k_gain, w_o):
    B, S, _ = h.shape
    qkv = (h @ w_qkv).reshape(B, S, N_GROUPS, 3, A_HEADS, A_HEAD_DIM)
    outs, lses = [], []
    for g, (window, dilation) in enumerate(SWA_GROUPS):
        q = apply_partial_rope(rmsnorm(qkv[:, :, g, 0], q_gain[g]), cos, sin)
        k = apply_partial_rope(rmsnorm(qkv[:, :, g, 1], k_gain[g]), cos, sin)
        o, lse = dilated_band_attention(q, k, qkv[:, :, g, 2], window, dilation)
        outs.append(o)
        lses.append(lse)
    wts = jax.nn.softmax(jnp.stack(lses, axis=0), axis=0)
    o = jnp.einsum('gbsh,gbshe->bshe', wts.astype(h.dtype), jnp.stack(outs, axis=0))
    return o.reshape(B, S, A_OUT_WIDTH) @ w_o


def causal_depthwise_conv(x, w):
    C = x.shape[-1]
    return lax.conv_general_dilated(
        x, w[:, None, :].astype(x.dtype), window_strides=(1,),
        padding=[(CONV_WIDTH - 1, 0)], dimension_numbers=('NWC', 'WIO', 'NWC'),
        feature_group_count=C)


def to_chunks(t):
    B, S = t.shape[:2]
    t = t.reshape(B, S // CHUNK, CHUNK, *t.shape[2:])
    return jnp.moveaxis(t, 2, 3)


def chunk_gated_delta_rule(q, k, v, g, beta):
    B, S, H, K = q.shape
    V = v.shape[-1]
    qc, kc, vc = to_chunks(q), to_chunks(k), to_chunks(v)
    gc = jnp.cumsum(to_chunks(g), axis=-1)
    bc = to_chunks(beta)[..., None]
    lower = jnp.tril(jnp.ones((CHUNK, CHUNK), dtype=bool))
    strict = jnp.tril(jnp.ones((CHUNK, CHUNK), dtype=bool), -1)
    decay = jnp.exp(jnp.where(lower, gc[..., :, None] - gc[..., None, :], -jnp.inf))
    kb = kc * bc
    a_mat = jnp.where(strict, jnp.einsum('bnhik,bnhjk->bnhij', kb, kc) * decay, 0.0)
    rhs = jnp.concatenate([vc * bc, kb * jnp.exp(gc)[..., None]], axis=-1)
    sol = lax.linalg.triangular_solve(a_mat + jnp.eye(CHUNK, dtype=a_mat.dtype), rhs,
                                      left_side=True, lower=True, unit_diagonal=True)
    u, w = sol[..., :V], sol[..., V:]
    attn = jnp.einsum('bnhik,bnhjk->bnhij', qc, kc) * decay
    q_dec = qc * jnp.exp(gc)[..., None]
    k_dec = kc * jnp.exp(gc[..., -1:] - gc)[..., None]
    c_dec = jnp.exp(gc[..., -1])

    def step(state, inp):
        u_i, w_i, qd_i, kd_i, at_i, cd_i = inp
        v_new = u_i - jnp.einsum('bhck,bhkv->bhcv', w_i, state)
        o_i = jnp.einsum('bhck,bhkv->bhcv', qd_i, state) + jnp.einsum('bhij,bhjv->bhiv', at_i, v_new)
        state = state * cd_i[..., None, None] + jnp.einsum('bhck,bhcv->bhkv', kd_i, v_new)
        return state, o_i

    xs = tuple(jnp.moveaxis(t, 1, 0) for t in (u, w, q_dec, k_dec, attn, c_dec))
    state0 = jnp.zeros((B, H, K, V), dtype=jnp.float32)
    _, o = lax.scan(step, state0, xs)
    return jnp.moveaxis(o, 0, 1).swapaxes(2, 3).reshape(B, S, H, V)


def gated_deltanet_mixer(h, w_in, conv_w, a_log, dt_bias, o_gain, w_o):
    B, S, _ = h.shape
    proj = h @ w_in
    c = 3 * DN_WIDTH
    qkv = proj[..., :c]
    a = proj[..., c:c + DN_HEADS]
    b = proj[..., c + DN_HEADS:c + 2 * DN_HEADS]
    z = proj[..., c + 2 * DN_HEADS:]
    qkv = jax.nn.silu(causal_depthwise_conv(qkv, conv_w))
    q, k, v = jnp.split(qkv, 3, axis=-1)
    q = l2norm(q.reshape(B, S, DN_HEADS, DN_HEAD_DIM)) * (DN_HEAD_DIM ** -0.5)
    k = l2norm(k.reshape(B, S, DN_HEADS, DN_HEAD_DIM))
    v = v.reshape(B, S, DN_HEADS, DN_HEAD_DIM).astype(jnp.float32)
    beta = jax.nn.sigmoid(b.astype(jnp.float32))
    g = -jnp.exp(a_log.astype(jnp.float32)) * jax.nn.softplus(a.astype(jnp.float32) + dt_bias.astype(jnp.float32))
    o = chunk_gated_delta_rule(q, k, v, g, beta)
    o = rmsnorm(o, o_gain) * jax.nn.silu(z.reshape(B, S, DN_HEADS, DN_HEAD_DIM).astype(jnp.float32))
    return o.reshape(B, S, DN_WIDTH).astype(h.dtype) @ w_o


def setup_inputs(seed: int = 0) -> dict:
    key = jax.random.key(seed)
    ks = jax.random.split(key, 20)
    NA, NB = N_ATTN_LAYERS, N_DELTA_LAYERS

    def nrm(k, shape, fan_in):
        return jax.random.normal(k, shape, jnp.float32) * (fan_in ** -0.5)

    def gain(k, shape):
        return 1.0 + 0.02 * jax.random.normal(k, shape, jnp.float32)

    dt = jnp.exp(jax.random.uniform(ks[9], (NB, DN_HEADS), jnp.float32,
                                    minval=math.log(1e-3), maxval=math.log(1e-1)))
    return {
        'x': jax.random.normal(ks[0], (BATCH, SEQ, D_MODEL), jnp.float32),
        'p': jax.random.normal(ks[1], (DEPTH, BATCH, SEQ, PLE_DIM), jnp.float32),
        'positions': jnp.broadcast_to(jnp.arange(SEQ, dtype=jnp.int32), (BATCH, SEQ)),
        'mix_norm': gain(ks[2], (DEPTH, D_MODEL)),
        'attn_w_qkv': nrm(ks[3], (NA, D_MODEL, A_QKV_WIDTH), D_MODEL),
        'attn_q_gain': gain(ks[4], (NA, N_GROUPS, A_HEAD_DIM)),
        'attn_k_gain': gain(ks[5], (NA, N_GROUPS, A_HEAD_DIM)),
        'attn_w_o': nrm(ks[6], (NA, A_OUT_WIDTH, D_MODEL), A_OUT_WIDTH),
        'dn_w_in': nrm(ks[7], (NB, D_MODEL, DN_IN_WIDTH), D_MODEL),
        'dn_conv': nrm(ks[8], (NB, CONV_WIDTH, 3 * DN_WIDTH), CONV_WIDTH),
        'dn_a_log': jnp.log(jax.random.uniform(ks[10], (NB, DN_HEADS), jnp.float32, minval=1.0, maxval=16.0)),
        'dn_dt_bias': dt + jnp.log(-jnp.expm1(-dt)),
        'dn_o_gain': gain(ks[11], (NB, DN_HEAD_DIM)),
        'dn_w_o': nrm(ks[12], (NB, DN_WIDTH, D_MODEL), DN_WIDTH),
        'mlp_norm': gain(ks[13], (DEPTH, D_MODEL)),
        'w_up': nrm(ks[14], (DEPTH, D_MODEL, D_FF), D_MODEL),
        'w_down': nrm(ks[15], (DEPTH, D_FF, D_MODEL), D_FF),
        'ple_norm': gain(ks[16], (DEPTH, D_MODEL)),
        'w_ple': nrm(ks[17], (DEPTH, PLE_DIM, D_MODEL), PLE_DIM),
        'w_ple_gate': nrm(ks[18], (DEPTH, D_MODEL, D_MODEL), D_MODEL),
    }


def reference(x, p, positions, mix_norm, attn_w_qkv, attn_q_gain, attn_k_gain, attn_w_o,
              dn_w_in, dn_conv, dn_a_log, dn_dt_bias, dn_o_gain, dn_w_o,
              mlp_norm, w_up, w_down, ple_norm, w_ple, w_ple_gate):
    cos, sin = rope_tables(positions)
    for i in range(DEPTH):
        j = i // N_MIXERS
        hn = rmsnorm(x, mix_norm[i])
        if i % N_MIXERS == 0:
            mix = dilated_attention_mixer(hn, cos, sin, attn_w_qkv[j], attn_q_gain[j],
                                          attn_k_gain[j], attn_w_o[j])
        else:
            mix = gated_deltanet_mixer(hn, dn_w_in[j], dn_conv[j], dn_a_log[j], dn_dt_bias[j],
                                       dn_o_gain[j], dn_w_o[j])
        x = x + mix
        hn = rmsnorm(x, mlp_norm[i])
        x = x + jnp.square(jax.nn.relu(hn @ w_up[i])) @ w_down[i]
        gate = jax.nn.sigmoid((rmsnorm(x, ple_norm[i]) @ w_ple_gate[i]).astype(jnp.float32)).astype(x.dtype)
        x = x + (p[i] @ w_ple[i]) * gate
    return x
```

```python
import functools
import math

import jax
import jax.numpy as jnp
from jax import lax
from jax.experimental import pallas as pl
from jax.experimental.pallas import tpu as pltpu

F32 = jnp.float32
BF16 = jnp.bfloat16

EPS = 1e-6
SWA_GROUPS = ((128, 1), (512, 4), (2048, 16))
A_HEADS = 8
A_HEAD_DIM = 64
A_WIDTH = A_HEADS * A_HEAD_DIM
ROPE_DIM = A_HEAD_DIM // 4
ROPE_HALF = ROPE_DIM // 2
ROPE_THETA = 500000.0
BAND = 128
DN_HEADS = 8
DN_HEAD_DIM = 128
DN_WIDTH = DN_HEADS * DN_HEAD_DIM
CONV_WIDTH = 4
CHUNK = 64
NEG_BIG = -1e30

V7X_LANES = 128
V7X_SUBLANES = 8
V7X_VMEM_LIMIT = 56 * 1024 * 1024


def _params(semantics):
    return pltpu.CompilerParams(dimension_semantics=semantics, vmem_limit_bytes=V7X_VMEM_LIMIT)


def _rms(x, gain):
    ms = jnp.mean(x * x, axis=-1, keepdims=True)
    return x * lax.rsqrt(ms + EPS) * gain


def _dot(a, b):
    return jnp.dot(a, b, preferred_element_type=F32)


def _dot_nt(a, b):
    return lax.dot_general(a, b, (((1,), (1,)), ((), ())), preferred_element_type=F32)


def _dot_tn(a, b):
    return lax.dot_general(a, b, (((0,), (0,)), ((), ())), preferred_element_type=F32)


def _rope_tables_kernel(pos_ref, invf_ref, c_ref, s1_ref, s2_ref):
    ang = pos_ref[...].astype(F32) * invf_ref[...]
    e = lax.broadcasted_iota(jnp.int32, ang.shape, 1) & (A_HEAD_DIM - 1)
    cos = jnp.cos(ang)
    sin = jnp.sin(ang)
    c_ref[...] = jnp.where(e < ROPE_DIM, cos, 1.0)
    s1_ref[...] = jnp.where(e < ROPE_HALF, -sin, 0.0)
    s2_ref[...] = jnp.where(e < ROPE_HALF, 0.0, jnp.where(e < ROPE_DIM, sin, 0.0))


def _rope_tables(positions):
    T = positions.size
    tm = min(T, 2048)
    inv_freq = ROPE_THETA ** (-jnp.arange(0, ROPE_DIM, 2, dtype=F32) / ROPE_DIM)
    invf = jnp.tile(inv_freq, V7X_LANES // ROPE_HALF)[None, :]
    out = jax.ShapeDtypeStruct((T, V7X_LANES), F32)
    spec = pl.BlockSpec((tm, V7X_LANES), lambda i: (i, 0))
    return pl.pallas_call(
        _rope_tables_kernel,
        grid=(T // tm,),
        in_specs=[pl.BlockSpec((tm, 1), lambda i: (i, 0)),
                  pl.BlockSpec((1, V7X_LANES), lambda i: (0, 0))],
        out_specs=[spec, spec, spec],
        out_shape=[out, out, out],
        compiler_params=_params(("parallel",)),
        name="rope_tables",
    )(positions.reshape(T, 1), invf)


def _qkv_kernel(x_ref, g_ref, w_ref, qg_ref, kg_ref, c_ref, s1_ref, s2_ref, bd_ref,
                q_ref, k_ref, v_ref):
    hn = _rms(x_ref[0], g_ref[...]).astype(BF16)
    acc = _dot(hn, w_ref[...])
    reps = A_WIDTH // V7X_LANES
    c = jnp.concatenate([c_ref[0]] * reps, axis=1)
    s1 = jnp.concatenate([s1_ref[0]] * reps, axis=1)
    s2 = jnp.concatenate([s2_ref[0]] * reps, axis=1)

    def norm_rope(y, gain):
        ss = _dot((y * y).astype(BF16), bd_ref[...])
        yn = y * lax.rsqrt(ss * (1.0 / A_HEAD_DIM) + EPS) * gain
        hi = pltpu.roll(yn, A_WIDTH - ROPE_HALF, 1)
        lo = pltpu.roll(yn, ROPE_HALF, 1)
        return yn * c + hi * s1 + lo * s2

    q = norm_rope(acc[:, :A_WIDTH], qg_ref[...]) * (A_HEAD_DIM ** -0.5)
    k = norm_rope(acc[:, A_WIDTH:2 * A_WIDTH], kg_ref[...])
    q_ref[0, 0] = q.astype(BF16)
    k_ref[0, 0] = k.astype(BF16)
    v_ref[0, 0] = acc[:, 2 * A_WIDTH:].astype(BF16)


def _qkv_group(x, gain, w_g, q_gain, k_gain, tables, bd, dilation):
    B, S, D = x.shape
    L = S // dilation
    tm = min(L, 512)
    xv = x.reshape(B, L, dilation * D)
    tv = [t.reshape(B, L, dilation * V7X_LANES) for t in tables]
    out = jax.ShapeDtypeStruct((B, dilation, L, A_WIDTH), BF16)
    ospec = pl.BlockSpec((1, 1, tm, A_WIDTH), lambda b, r, n: (b, r, n, 0))
    tspec = pl.BlockSpec((1, tm, V7X_LANES), lambda b, r, n: (b, n, r))
    const = lambda shape: pl.BlockSpec(shape, lambda b, r, n: (0,) * len(shape))
    return pl.pallas_call(
        _qkv_kernel,
        grid=(B, dilation, L // tm),
        in_specs=[pl.BlockSpec((1, tm, D), lambda b, r, n: (b, n, r)),
                  const((1, D)), const((D, 3 * A_WIDTH)), const((1, A_WIDTH)), const((1, A_WIDTH)),
                  tspec, tspec, tspec, const((A_WIDTH, A_WIDTH))],
        out_specs=[ospec, ospec, ospec],
        out_shape=[out, out, out],
        compiler_params=_params(("parallel", "parallel", "parallel")),
        name=f"attn_qkv_d{dilation}",
    )(xv, gain, w_g, q_gain, k_gain, *tv, bd)


def _attn_kernel(q_ref, kc_ref, kp_ref, vc_ref, vp_ref, o_ref, lse_ref, kbuf, vbuf, *, tq):
    n = pl.program_id(2)
    kbuf[0:BAND] = kp_ref[0, 0]
    kbuf[BAND:] = kc_ref[0, 0]
    vbuf[0:BAND] = vp_ref[0, 0]
    vbuf[BAND:] = vc_ref[0, 0]

    qi = lax.broadcasted_iota(jnp.int32, (BAND, 2 * BAND), 0)
    kj = lax.broadcasted_iota(jnp.int32, (BAND, 2 * BAND), 1)
    band = jnp.where(kj >= qi, jnp.where(kj <= qi + BAND, 0.0, NEG_BIG), NEG_BIG)
    first_lo = jnp.where(n > 0, 0, BAND)
    band_first = jnp.where(kj >= first_lo, band, NEG_BIG)
    lane = lax.broadcasted_iota(jnp.int32, (BAND, V7X_LANES), 1)
    even = lane < A_HEAD_DIM
    lane_row = lax.broadcasted_iota(jnp.int32, (1, V7X_LANES), 1)
    head_masks = (jnp.where(lane_row < A_HEAD_DIM, 1.0, 0.0).astype(BF16),
                  jnp.where(lane_row < A_HEAD_DIM, 0.0, 1.0).astype(BF16))

    for j in range(tq // BAND):
        bias = band_first if j == 0 else band
        rows = slice(j * BAND, (j + 1) * BAND)
        for p in range(A_WIDTH // V7X_LANES):
            cols = slice(p * V7X_LANES, (p + 1) * V7X_LANES)
            qp = q_ref[0, 0, rows, cols]
            kk = kbuf[j * BAND:(j + 2) * BAND, cols]
            vv = vbuf[j * BAND:(j + 2) * BAND, cols]
            outs, lses = [], []
            for hm in head_masks:
                s = _dot_nt(qp * hm, kk) + bias
                m = jnp.max(s, axis=-1, keepdims=True)
                e = jnp.exp(s - m)
                l = jnp.sum(e, axis=-1, keepdims=True)
                o = _dot(e.astype(BF16), vv)
                outs.append(o * (1.0 / l))
                lses.append(m + jnp.log(l))
            o_ref[0, rows, cols] = jnp.where(even, outs[0], outs[1]).astype(BF16)
            lse_ref[0, rows, cols] = jnp.where(even, lses[0], lses[1])


def _attn_group(q, k, v, dilation):
    B, d, L, W = q.shape
    tq = min(L, 512)
    nsub = tq // BAND
    cur = pl.BlockSpec((1, 1, tq, W), lambda b, r, n: (b, r, n, 0))
    prev = pl.BlockSpec((1, 1, BAND, W), lambda b, r, n: (b, r, jnp.maximum(n * nsub - 1, 0), 0))
    ospec = pl.BlockSpec((1, tq, W), lambda b, r, n: (b, n, r))
    o, lse = pl.pallas_call(
        functools.partial(_attn_kernel, tq=tq),
        grid=(B, d, L // tq),
        in_specs=[cur, cur, prev, cur, prev],
        out_specs=[ospec, ospec],
        out_shape=[jax.ShapeDtypeStruct((B, L, d * W), BF16),
                   jax.ShapeDtypeStruct((B, L, d * W), F32)],
        scratch_shapes=[pltpu.VMEM((BAND + tq, W), BF16), pltpu.VMEM((BAND + tq, W), BF16)],
        compiler_params=_params(("parallel", "parallel", "arbitrary")),
        name=f"band_attn_d{dilation}",
    )(q, k, k, v, v)
    return o.reshape(B * L * d, W), lse.reshape(B * L * d, W)


def _merge_out_kernel(o0, o1, o2, l0, l1, l2, x_ref, w_ref, out_ref):
    a0, a1, a2 = l0[...], l1[...], l2[...]
    m = jnp.maximum(jnp.maximum(a0, a1), a2)
    e0, e1, e2 = jnp.exp(a0 - m), jnp.exp(a1 - m), jnp.exp(a2 - m)
    inv = 1.0 / (e0 + e1 + e2)
    merged = (e0 * inv) * o0[...].astype(F32) + (e1 * inv) * o1[...].astype(F32) + (e2 * inv) * o2[...].astype(F32)
    out_ref[...] = x_ref[...] + _dot(merged.astype(BF16), w_ref[...])


def _merge_out(os, ls, x2d, w_o):
    T, D = x2d.shape
    tm = min(T, 1024)
    ospec = pl.BlockSpec((tm, A_WIDTH), lambda i: (i, 0))
    xspec = pl.BlockSpec((tm, D), lambda i: (i, 0))
    return pl.pallas_call(
        _merge_out_kernel,
        grid=(T // tm,),
        in_specs=[ospec] * 6 + [xspec, pl.BlockSpec((A_WIDTH, D), lambda i: (0, 0))],
        out_specs=xspec,
        out_shape=jax.ShapeDtypeStruct((T, D), F32),
        compiler_params=_params(("parallel",)),
        name="attn_merge_out",
    )(*os, *ls, x2d, w_o)


def _proj_residual_kernel(a_ref, x_ref, w_ref, out_ref):
    out_ref[...] = x_ref[...] + _dot(a_ref[...], w_ref[...])


def _proj_residual(a, x2d, w):
    T, D = x2d.shape
    K = a.shape[1]
    tm = min(T, 1024)
    xspec = pl.BlockSpec((tm, D), lambda i: (i, 0))
    return pl.pallas_call(
        _proj_residual_kernel,
        grid=(T // tm,),
        in_specs=[pl.BlockSpec((tm, K), lambda i: (i, 0)), xspec, pl.BlockSpec((K, D), lambda i: (0, 0))],
        out_specs=xspec,
        out_shape=jax.ShapeDtypeStruct((T, D), F32),
        compiler_params=_params(("parallel",)),
        name="proj_residual",
    )(a, x2d, w)


def _mlp_kernel(x_ref, g_ref, wup_ref, wdn_ref, p_ref, g2_ref, wple_ref, wgate_ref, out_ref, hn_scr, acc_scr):
    j = pl.program_id(1)

    @pl.when(j == 0)
    def _():
        hn_scr[...] = _rms(x_ref[...], g_ref[...]).astype(BF16)
        acc_scr[...] = x_ref[...]

    h = jnp.maximum(_dot(hn_scr[...], wup_ref[...]), 0.0)
    acc_scr[...] += _dot((h * h).astype(BF16), wdn_ref[...])

    @pl.when(j == pl.num_programs(1) - 1)
    def _():
        x2 = acc_scr[...]
        hn2 = _rms(x2, g2_ref[...]).astype(BF16)
        gate = jax.nn.sigmoid(_dot(hn2, wgate_ref[...]))
        ple = _dot(p_ref[...].astype(BF16), wple_ref[...])
        out_ref[...] = x2 + ple * gate


def _mlp(x2d, gain, w_up, w_down, p2d, gain2, w_ple, w_gate):
    T, D = x2d.shape
    FF = w_up.shape[1]
    P = p2d.shape[1]
    tm = min(T, 1024)
    tf = min(FF, 1024)
    xspec = pl.BlockSpec((tm, D), lambda i, j: (i, 0))
    const = lambda shape: pl.BlockSpec(shape, lambda i, j: (0, 0))
    return pl.pallas_call(
        _mlp_kernel,
        grid=(T // tm, FF // tf),
        in_specs=[xspec, const((1, D)),
                  pl.BlockSpec((D, tf), lambda i, j: (0, j)),
                  pl.BlockSpec((tf, D), lambda i, j: (j, 0)),
                  pl.BlockSpec((tm, P), lambda i, j: (i, 0)),
                  const((1, D)), const((P, D)), const((D, D))],
        out_specs=xspec,
        out_shape=jax.ShapeDtypeStruct((T, D), F32),
        scratch_shapes=[pltpu.VMEM((tm, D), BF16), pltpu.VMEM((tm, D), F32)],
        compiler_params=_params(("parallel", "arbitrary")),
        name="channel_mixer",
    )(x2d, gain, w_up, w_down, p2d, gain2, w_ple, w_gate)


def _seg_cumsum(x, axis, reverse=False):
    n = x.shape[axis]
    pos = lax.broadcasted_iota(jnp.int32, x.shape, axis) & (CHUNK - 1)
    step = 1
    if reverse:
        acc = jnp.where(pos < CHUNK - 1, pltpu.roll(x, n - 1, axis), 0.0)
        x = acc
        while step < CHUNK:
            x = x + jnp.where(pos < CHUNK - step, pltpu.roll(x, n - step, axis), 0.0)
            step *= 2
        return x
    while step < CHUNK:
        x = x + jnp.where(pos >= step, pltpu.roll(x, step, axis), 0.0)
        step *= 2
    return x


def _softplus(x):
    return jnp.maximum(x, 0.0) + jnp.log1p(jnp.exp(-jnp.abs(x)))


def _dn_proj_kernel(x_ref, g_ref, wqkv_ref, wz_ref, wab_ref, wabt_ref, conv_ref,
                    alog_r, dtb_r, alog_c, dtb_c,
                    k_ref, kb_ref, vb_ref, kbg_ref, q_ref, qd_ref, kd_ref, z_ref, gcol_ref, grow_ref,
                    y_scr, *, tm):
    s = pl.program_id(1)
    H, E = DN_HEADS, DN_HEAD_DIM
    PAD = V7X_SUBLANES

    @pl.when(s == 0)
    def _():
        y_scr[0:PAD] = jnp.zeros((PAD, 3 * DN_WIDTH), F32)

    hn = _rms(x_ref[0], g_ref[...]).astype(BF16)
    y_scr[PAD:] = _dot(hn, wqkv_ref[...])
    z_ref[0] = _dot(hn, wz_ref[...]).astype(BF16)
    ab = _dot(hn, wab_ref[...])
    abt = _dot_nt(wabt_ref[...], hn)

    g_col = -jnp.exp(alog_r[...]) * _softplus(ab[:, :V7X_LANES] + dtb_r[...])
    beta_col = jax.nn.sigmoid(ab[:, V7X_LANES:])
    gam_col = _seg_cumsum(g_col, 0)
    suf_col = _seg_cumsum(g_col, 0, reverse=True)
    eg_col = jnp.exp(gam_col)
    esuf_col = jnp.exp(suf_col)
    gcol_ref[0] = gam_col
    g_row = -jnp.exp(alog_c[...]) * _softplus(abt[:H] + dtb_c[...])
    grow_ref[0] = _seg_cumsum(g_row, 1)

    def conv_silu(cb):
        cols = slice(cb * E, (cb + 1) * E)
        y = conv_ref[0:1, cols] * y_scr[PAD - 3:PAD - 3 + tm, cols]
        for j in range(1, CONV_WIDTH):
            y = y + conv_ref[j:j + 1, cols] * y_scr[PAD - 3 + j:PAD - 3 + j + tm, cols]
        return y * jax.nn.sigmoid(y)

    def l2n(y):
        return y * lax.rsqrt(jnp.sum(y * y, axis=-1, keepdims=True) + EPS)

    for h in range(H):
        cols = slice(h * E, (h + 1) * E)
        q = l2n(conv_silu(h)) * (E ** -0.5)
        k = l2n(conv_silu(H + h))
        v = conv_silu(2 * H + h)
        beta = beta_col[:, h:h + 1]
        eg = eg_col[:, h:h + 1]
        kb = k * beta
        k_ref[0, :, cols] = k.astype(BF16)
        kb_ref[0, :, cols] = kb.astype(BF16)
        vb_ref[0, :, cols] = (v * beta).astype(BF16)
        kbg_ref[0, :, cols] = (kb * eg).astype(BF16)
        q_ref[0, :, cols] = q.astype(BF16)
        qd_ref[0, :, cols] = (q * eg).astype(BF16)
        kd_ref[0, :, cols] = (k * esuf_col[:, h:h + 1]).astype(BF16)

    y_scr[0:PAD] = y_scr[tm:tm + PAD]


def _dn_proj(x, gain, w_qkv, w_z, w_ab, w_abt, conv_w, alog_r, dtb_r, alog_c, dtb_c):
    B, S, D = x.shape
    tm = min(S, 256)
    W = DN_WIDTH
    big = jax.ShapeDtypeStruct((B, S, W), BF16)
    bspec = pl.BlockSpec((1, tm, W), lambda b, s: (b, s, 0))
    const = lambda shape: pl.BlockSpec(shape, lambda b, s: (0,) * len(shape))
    return pl.pallas_call(
        functools.partial(_dn_proj_kernel, tm=tm),
        grid=(B, S // tm),
        in_specs=[pl.BlockSpec((1, tm, D), lambda b, s: (b, s, 0)), const((1, D)),
                  const((D, 3 * W)), const((D, W)), const((D, 2 * V7X_LANES)), const((2 * DN_HEADS, D)),
                  const((CONV_WIDTH, 3 * W)),
                  const((1, V7X_LANES)), const((1, V7X_LANES)), const((DN_HEADS, 1)), const((DN_HEADS, 1))],
        out_specs=[bspec] * 8 + [pl.BlockSpec((1, tm, V7X_LANES), lambda b, s: (b, s, 0)),
                                 pl.BlockSpec((1, DN_HEADS, tm), lambda b, s: (b, 0, s))],
        out_shape=[big] * 8 + [jax.ShapeDtypeStruct((B, S, V7X_LANES), F32),
                               jax.ShapeDtypeStruct((B, DN_HEADS, S), F32)],
        scratch_shapes=[pltpu.VMEM((tm + V7X_SUBLANES, 3 * W), F32)],
        compiler_params=_params(("parallel", "arbitrary")),
        name="deltanet_proj",
    )(x, gain, w_qkv, w_z, w_ab, w_abt, conv_w, alog_r, dtb_r, alog_c, dtb_c)


def _delta_kernel(k_ref, kb_ref, vb_ref, kbg_ref, q_ref, qd_ref, kd_ref, z_ref, gcol_ref, grow_ref, og_ref,
                  out_ref, state, *, tc):
    h = pl.program_id(1)
    n = pl.program_id(2)
    C = CHUNK

    @pl.when(n == 0)
    def _():
        state[...] = jnp.zeros_like(state)

    lane = lax.broadcasted_iota(jnp.int32, (tc, V7X_LANES), 1)
    gam_c = jnp.sum(jnp.where(lane == h, gcol_ref[0], 0.0), axis=-1, keepdims=True)
    gam_r = grow_ref[0, pl.ds(h, 1), :]
    ri = lax.broadcasted_iota(jnp.int32, (C, C), 0)
    ci = lax.broadcasted_iota(jnp.int32, (C, C), 1)
    lower = ri >= ci
    strict = ri > ci

    S = state[...]
    for c in range(tc // C):
        rows = slice(c * C, (c + 1) * C)
        kc = k_ref[0, rows, :]
        gc = gam_c[rows]
        diff = gc - gam_r[:, rows]
        decay = jnp.exp(jnp.where(lower, diff, NEG_BIG))
        gram = _dot_nt(jnp.concatenate([q_ref[0, rows, :], kb_ref[0, rows, :]], axis=0), kc)
        attn = gram[:C] * decay
        a_mat = jnp.where(strict, gram[C:] * decay, 0.0)
        pw = -a_mat
        nm = pw
        for _ in range(int(math.log2(C)) - 1):
            pwb = pw.astype(BF16)
            pw = _dot(pwb, pwb)
            nm = nm + pw + _dot(nm.astype(BF16), pw.astype(BF16))
        rhs = jnp.concatenate([vb_ref[0, rows, :], kbg_ref[0, rows, :]], axis=1)
        sol = rhs.astype(F32) + _dot(nm.astype(BF16), rhs)
        u = sol[:, :DN_HEAD_DIM]
        w = sol[:, DN_HEAD_DIM:]
        ws = _dot(jnp.concatenate([w.astype(BF16), qd_ref[0, rows, :]], axis=0), S.astype(BF16))
        v_new = u - ws[:C]
        vnb = v_new.astype(BF16)
        o = ws[C:] + _dot(attn.astype(BF16), vnb)
        c_dec = jnp.exp(gc[C - 1:C, :])
        S = S * c_dec + _dot_tn(kd_ref[0, rows, :], vnb)
        on = _rms(o, og_ref[...])
        zc = z_ref[0, rows, :].astype(F32)
        out_ref[0, rows, :] = (on * (zc * jax.nn.sigmoid(zc))).astype(BF16)
    state[...] = S


def _delta_rule(k, kb, vb, kbg, q, qd, kd, z, gcol, grow, o_gain):
    B, S, W = k.shape
    E = DN_HEAD_DIM
    tc = min(S, 512)
    hspec = pl.BlockSpec((1, tc, E), lambda b, h, n: (b, n, h))
    return pl.pallas_call(
        functools.partial(_delta_kernel, tc=tc),
        grid=(B, DN_HEADS, S // tc),
        in_specs=[hspec] * 8 + [pl.BlockSpec((1, tc, V7X_LANES), lambda b, h, n: (b, n, 0)),
                                pl.BlockSpec((1, DN_HEADS, tc), lambda b, h, n: (b, 0, n)),
                                pl.BlockSpec((1, E), lambda b, h, n: (0, 0))],
        out_specs=hspec,
        out_shape=jax.ShapeDtypeStruct((B, S, W), BF16),
        scratch_shapes=[pltpu.VMEM((E, E), F32)],
        compiler_params=_params(("parallel", "parallel", "arbitrary")),
        name="delta_rule",
    )(k, kb, vb, kbg, q, qd, kd, z, gcol, grow, o_gain)


def _attention_layer(x, tables, mix_gain, w_qkv, q_gain, k_gain, w_o):
    B, S, D = x.shape
    gain = mix_gain[None, :]
    bd = jnp.kron(jnp.eye(A_HEADS, dtype=F32), jnp.ones((A_HEAD_DIM, A_HEAD_DIM), F32)).astype(BF16)
    w_qkv = w_qkv.astype(BF16)
    os, ls = [], []
    for g, (window, dilation) in enumerate(SWA_GROUPS):
        assert window // dilation == BAND
        w_g = w_qkv[:, g * 3 * A_WIDTH:(g + 1) * 3 * A_WIDTH]
        qg = jnp.tile(q_gain[g], A_HEADS)[None, :]
        kg = jnp.tile(k_gain[g], A_HEADS)[None, :]
        q, k, v = _qkv_group(x, gain, w_g, qg, kg, tables, bd, dilation)
        o, lse = _attn_group(q, k, v, dilation)
        os.append(o)
        ls.append(lse)
    return _merge_out(os, ls, x.reshape(B * S, D), w_o.astype(BF16)).reshape(B, S, D)


def _deltanet_layer(x, mix_gain, w_in, conv_w, a_log, dt_bias, o_gain, w_o):
    B, S, D = x.shape
    W, H = DN_WIDTH, DN_HEADS
    c = 3 * W
    w_qkv = w_in[:, :c].astype(BF16)
    w_a = w_in[:, c:c + H]
    w_b = w_in[:, c + H:c + 2 * H]
    w_z = w_in[:, c + 2 * H:].astype(BF16)
    pad = lambda w: jnp.pad(w, ((0, 0), (0, V7X_LANES - H)))
    w_ab = jnp.concatenate([pad(w_a), pad(w_b)], axis=1).astype(BF16)
    w_abt = jnp.concatenate([w_a, w_b], axis=1).T.astype(BF16)
    padr = lambda v: jnp.pad(v, (0, V7X_LANES - H))[None, :]
    outs = _dn_proj(x, mix_gain[None, :], w_qkv, w_z, w_ab, w_abt, conv_w,
                    padr(a_log), padr(dt_bias), a_log[:, None], dt_bias[:, None])
    o = _delta_rule(*outs, o_gain[None, :])
    return _proj_residual(o.reshape(B * S, W), x.reshape(B * S, D), w_o.astype(BF16)).reshape(B, S, D)


def kernel(x, p, positions, mix_norm, attn_w_qkv, attn_q_gain, attn_k_gain, attn_w_o, dn_w_in, dn_conv,
           dn_a_log, dn_dt_bias, dn_o_gain, dn_w_o, mlp_norm, w_up, w_down, ple_norm, w_ple, w_ple_gate):
    B, S, D = x.shape
    depth = p.shape[0]
    tables = _rope_tables(positions)
    tables = [t.reshape(B, S, V7X_LANES) for t in tables]
    for i in range(depth):
        j = i // 2
        if i % 2 == 0:
            x = _attention_layer(x, tables, mix_norm[i], attn_w_qkv[j], attn_q_gain[j], attn_k_gain[j],
                                 attn_w_o[j])
        else:
            x = _deltanet_layer(x, mix_norm[i], dn_w_in[j], dn_conv[j], dn_a_log[j], dn_dt_bias[j],
                                dn_o_gain[j], dn_w_o[j])
        x = _mlp(x.reshape(B * S, D), mlp_norm[i][None, :], w_up[i].astype(BF16), w_down[i].astype(BF16),
                 p[i].reshape(B * S, -1), ple_norm[i][None, :], w_ple[i].astype(BF16),
                 w_ple_gate[i].astype(BF16)).reshape(B, S, D)
    return x
```

```python
import functools
import math

import jax
import jax.numpy as jnp
from jax import lax
from jax.experimental import pallas as pl
from jax.experimental.pallas import tpu as pltpu

F32 = jnp.float32
BF16 = jnp.bfloat16

EPS = 1e-6
SWA_GROUPS = ((128, 1), (512, 4), (2048, 16))
A_HEADS = 8
A_HEAD_DIM = 64
A_WIDTH = A_HEADS * A_HEAD_DIM
ROPE_DIM = A_HEAD_DIM // 4
ROPE_HALF = ROPE_DIM // 2
ROPE_THETA = 500000.0
BAND = 128
DN_HEADS = 8
DN_HEAD_DIM = 128
DN_WIDTH = DN_HEADS * DN_HEAD_DIM
CONV_WIDTH = 4
CHUNK = 64
NEG_BIG = -1e30

V7X_LANES = 128
V7X_SUBLANES = 8
V7X_VMEM_LIMIT = 56 * 1024 * 1024


def _params(semantics):
    return pltpu.CompilerParams(dimension_semantics=semantics, vmem_limit_bytes=V7X_VMEM_LIMIT)


def _rms(x, gain):
    ms = jnp.mean(x * x, axis=-1, keepdims=True)
    return x * lax.rsqrt(ms + EPS) * gain


def _dot(a, b):
    return jnp.dot(a, b, preferred_element_type=F32)


def _dot_nt(a, b):
    return lax.dot_general(a, b, (((1,), (1,)), ((), ())), preferred_element_type=F32)


def _dot_tn(a, b):
    return lax.dot_general(a, b, (((0,), (0,)), ((), ())), preferred_element_type=F32)


def _rope_tables_kernel(pos_ref, invf_ref, c_ref, s1_ref, s2_ref):
    ang = pos_ref[...].astype(F32) * invf_ref[...]
    e = lax.broadcasted_iota(jnp.int32, ang.shape, 1) & (A_HEAD_DIM - 1)
    cos = jnp.cos(ang)
    sin = jnp.sin(ang)
    c_ref[...] = jnp.where(e < ROPE_DIM, cos, 1.0)
    s1_ref[...] = jnp.where(e < ROPE_HALF, -sin, 0.0)
    s2_ref[...] = jnp.where(e < ROPE_HALF, 0.0, jnp.where(e < ROPE_DIM, sin, 0.0))


def _rope_tables(positions):
    T = positions.size
    tm = min(T, 2048)
    inv_freq = ROPE_THETA ** (-jnp.arange(0, ROPE_DIM, 2, dtype=F32) / ROPE_DIM)
    invf = jnp.tile(inv_freq, V7X_LANES // ROPE_HALF)[None, :]
    out = jax.ShapeDtypeStruct((T, V7X_LANES), F32)
    spec = pl.BlockSpec((tm, V7X_LANES), lambda i: (i, 0))
    return pl.pallas_call(
        _rope_tables_kernel,
        grid=(T // tm,),
        in_specs=[pl.BlockSpec((tm, 1), lambda i: (i, 0)),
                  pl.BlockSpec((1, V7X_LANES), lambda i: (0, 0))],
        out_specs=[spec, spec, spec],
        out_shape=[out, out, out],
        compiler_params=_params(("parallel",)),
        name="rope_tables",
    )(positions.reshape(T, 1), invf)


def _by_residue(ref, tm, d):
    if d == 1:
        return ref[0]
    return jnp.concatenate([ref[0, pl.ds(r, tm // d, stride=d), :] for r in range(d)], axis=0)


def _qkv_kernel(*refs, tm, ncol):
    x_refs = refs[:ncol]
    g_ref, w_ref, qg_ref, kg_ref, c_ref, s1_ref, s2_ref, bd_ref = refs[ncol:ncol + 8]
    out_refs = refs[ncol + 8:]
    reps = A_WIDTH // V7X_LANES
    for gi, (_, d) in enumerate(SWA_GROUPS):
        q_ref, k_ref, v_ref = out_refs[3 * gi:3 * gi + 3]
        xg = jnp.concatenate([_by_residue(xr, tm, d) for xr in x_refs], axis=1)
        hn = _rms(xg, g_ref[...]).astype(BF16)
        acc = _dot(hn, w_ref[:, 3 * A_WIDTH * gi:3 * A_WIDTH * (gi + 1)])
        c = jnp.concatenate([_by_residue(c_ref, tm, d)] * reps, axis=1)
        s1 = jnp.concatenate([_by_residue(s1_ref, tm, d)] * reps, axis=1)
        s2 = jnp.concatenate([_by_residue(s2_ref, tm, d)] * reps, axis=1)

        def norm_rope(y, gain):
            ss = _dot((y * y).astype(BF16), bd_ref[...])
            yn = y * lax.rsqrt(ss * (1.0 / A_HEAD_DIM) + EPS) * gain
            hi = pltpu.roll(yn, A_WIDTH - ROPE_HALF, 1)
            lo = pltpu.roll(yn, ROPE_HALF, 1)
            return yn * c + hi * s1 + lo * s2

        q = (norm_rope(acc[:, :A_WIDTH], qg_ref[gi:gi + 1, :]) * (A_HEAD_DIM ** -0.5)).astype(BF16)
        k = norm_rope(acc[:, A_WIDTH:2 * A_WIDTH], kg_ref[gi:gi + 1, :]).astype(BF16)
        v = acc[:, 2 * A_WIDTH:].astype(BF16)
        n = tm // d
        for r in range(d):
            q_ref[0, r] = q[r * n:(r + 1) * n]
            k_ref[0, r] = k[r * n:(r + 1) * n]
            v_ref[0, r] = v[r * n:(r + 1) * n]


def _qkv_all(x, gain, w_qkv, q_gain, k_gain, tables, bd):
    B, S, D = x.shape
    tm = min(S, 512)
    G = len(SWA_GROUPS)
    tspec = pl.BlockSpec((1, tm, V7X_LANES), lambda b, n: (b, n, 0))
    const = lambda shape: pl.BlockSpec(shape, lambda b, n: (0,) * len(shape))
    out_specs, out_shape = [], []
    for _, d in SWA_GROUPS:
        out_specs += [pl.BlockSpec((1, d, tm // d, A_WIDTH), lambda b, n: (b, 0, n, 0))] * 3
        out_shape += [jax.ShapeDtypeStruct((B, d, S // d, A_WIDTH), BF16)] * 3
    ncol = D // V7X_LANES
    xspecs = [pl.BlockSpec((1, tm, V7X_LANES), functools.partial(lambda b, n, c: (b, n, c), c=c))
              for c in range(ncol)]
    outs = pl.pallas_call(
        functools.partial(_qkv_kernel, tm=tm, ncol=ncol),
        grid=(B, S // tm),
        in_specs=xspecs + [const((1, D)), const((D, G * 3 * A_WIDTH)), const((G, A_WIDTH)), const((G, A_WIDTH)),
                           tspec, tspec, tspec, const((A_WIDTH, A_WIDTH))],
        out_specs=out_specs,
        out_shape=out_shape,
        compiler_params=_params(("parallel", "parallel")),
        name="attn_qkv",
    )(*([x] * ncol), gain, w_qkv, q_gain, k_gain, *tables, bd)
    return [outs[3 * gi:3 * gi + 3] for gi in range(G)]


def _attn_kernel(q_ref, kc_ref, kp_ref, vc_ref, vp_ref, o_ref, lse_ref, kbuf, vbuf, *, tq):
    n = pl.program_id(2)
    kbuf[0:BAND] = kp_ref[0, 0]
    kbuf[BAND:] = kc_ref[0, 0]
    vbuf[0:BAND] = vp_ref[0, 0]
    vbuf[BAND:] = vc_ref[0, 0]

    qi = lax.broadcasted_iota(jnp.int32, (BAND, 2 * BAND), 0)
    kj = lax.broadcasted_iota(jnp.int32, (BAND, 2 * BAND), 1)
    band = jnp.where(kj >= qi, jnp.where(kj <= qi + BAND, 0.0, NEG_BIG), NEG_BIG)
    first_lo = jnp.where(n > 0, 0, BAND)
    band_first = jnp.where(kj >= first_lo, band, NEG_BIG)
    lane = lax.broadcasted_iota(jnp.int32, (BAND, V7X_LANES), 1)
    even = lane < A_HEAD_DIM
    lane_row = lax.broadcasted_iota(jnp.int32, (1, V7X_LANES), 1)
    head_masks = (jnp.where(lane_row < A_HEAD_DIM, 1.0, 0.0).astype(BF16),
                  jnp.where(lane_row < A_HEAD_DIM, 0.0, 1.0).astype(BF16))

    for j in range(tq // BAND):
        bias = band_first if j == 0 else band
        rows = slice(j * BAND, (j + 1) * BAND)
        for p in range(A_WIDTH // V7X_LANES):
            cols = slice(p * V7X_LANES, (p + 1) * V7X_LANES)
            qp = q_ref[0, 0, rows, cols]
            kk = kbuf[j * BAND:(j + 2) * BAND, cols]
            vv = vbuf[j * BAND:(j + 2) * BAND, cols]
            outs, lses = [], []
            for hm in head_masks:
                s = _dot_nt(qp * hm, kk) + bias
                m = jnp.max(s, axis=-1, keepdims=True)
                e = jnp.exp(s - m)
                l = jnp.sum(e, axis=-1, keepdims=True)
                o = _dot(e.astype(BF16), vv)
                outs.append(o * (1.0 / l))
                lses.append(m + jnp.log(l))
            o_ref[0, 0, rows, cols] = jnp.where(even, outs[0], outs[1]).astype(BF16)
            lse_ref[0, 0, rows, cols] = jnp.where(even, lses[0], lses[1])


def _attn_group(q, k, v, dilation):
    B, d, L, W = q.shape
    tq = min(L, 512)
    nsub = tq // BAND
    cur = pl.BlockSpec((1, 1, tq, W), lambda b, r, n: (b, r, n, 0))
    prev = pl.BlockSpec((1, 1, BAND, W), lambda b, r, n: (b, r, jnp.maximum(n * nsub - 1, 0), 0))
    return pl.pallas_call(
        functools.partial(_attn_kernel, tq=tq),
        grid=(B, d, L // tq),
        in_specs=[cur, cur, prev, cur, prev],
        out_specs=[cur, cur],
        out_shape=[jax.ShapeDtypeStruct((B, d, L, W), BF16),
                   jax.ShapeDtypeStruct((B, d, L, W), F32)],
        scratch_shapes=[pltpu.VMEM((BAND + tq, W), BF16), pltpu.VMEM((BAND + tq, W), BF16)],
        compiler_params=_params(("parallel", "parallel", "arbitrary")),
        name=f"band_attn_d{dilation}",
    )(q, k, k, v, v)


def _merge_out_kernel(*refs, tm):
    G = len(SWA_GROUPS)
    o_refs, l_refs = refs[:G], refs[G:2 * G]
    x_ref, w_ref, out_ref, o_scr, l_scr = refs[2 * G:]
    os, ls = [], []
    for gi, (_, d) in enumerate(SWA_GROUPS):
        if d == 1:
            os.append(o_refs[gi][0, 0].astype(F32))
            ls.append(l_refs[gi][0, 0])
            continue
        ncol = A_WIDTH // V7X_LANES
        for r in range(d):
            ov = o_refs[gi][0, r].astype(F32)
            lv = l_refs[gi][0, r]
            for c in range(ncol):
                cols = slice(c * V7X_LANES, (c + 1) * V7X_LANES)
                o_scr[gi * ncol + c, pl.ds(r, tm // d, stride=d), :] = ov[:, cols]
                l_scr[gi * ncol + c, pl.ds(r, tm // d, stride=d), :] = lv[:, cols]
        os.append(jnp.concatenate([o_scr[gi * ncol + c] for c in range(ncol)], axis=1))
        ls.append(jnp.concatenate([l_scr[gi * ncol + c] for c in range(ncol)], axis=1))
    m = functools.reduce(jnp.maximum, ls)
    es = [jnp.exp(a - m) for a in ls]
    inv = 1.0 / functools.reduce(lambda a, b: a + b, es)
    merged = functools.reduce(lambda a, b: a + b, [(e * inv) * o for e, o in zip(es, os)])
    out_ref[0] = x_ref[0] + _dot(merged.astype(BF16), w_ref[...])


def _merge_out(os, ls, x, w_o):
    B, S, D = x.shape
    tm = min(S, 512)
    G = len(SWA_GROUPS)
    gspecs = [pl.BlockSpec((1, d, tm // d, A_WIDTH), lambda b, n: (b, 0, n, 0)) for _, d in SWA_GROUPS]
    xspec = pl.BlockSpec((1, tm, D), lambda b, n: (b, n, 0))
    return pl.pallas_call(
        functools.partial(_merge_out_kernel, tm=tm),
        grid=(B, S // tm),
        in_specs=gspecs + gspecs + [xspec, pl.BlockSpec((A_WIDTH, D), lambda b, n: (0, 0))],
        out_specs=xspec,
        out_shape=jax.ShapeDtypeStruct((B, S, D), F32),
        scratch_shapes=[pltpu.VMEM((G * A_WIDTH // V7X_LANES, tm, V7X_LANES), F32)] * 2,
        compiler_params=_params(("parallel", "parallel")),
        name="attn_merge_out",
    )(*os, *ls, x, w_o)


def _proj_residual_kernel(a_ref, x_ref, w_ref, out_ref):
    out_ref[...] = x_ref[...] + _dot(a_ref[...], w_ref[...])


def _proj_residual(a, x2d, w):
    T, D = x2d.shape
    K = a.shape[1]
    tm = min(T, 1024)
    xspec = pl.BlockSpec((tm, D), lambda i: (i, 0))
    return pl.pallas_call(
        _proj_residual_kernel,
        grid=(T // tm,),
        in_specs=[pl.BlockSpec((tm, K), lambda i: (i, 0)), xspec, pl.BlockSpec((K, D), lambda i: (0, 0))],
        out_specs=xspec,
        out_shape=jax.ShapeDtypeStruct((T, D), F32),
        compiler_params=_params(("parallel",)),
        name="proj_residual",
    )(a, x2d, w)


def _mlp_kernel(x_ref, g_ref, wup_ref, wdn_ref, p_ref, g2_ref, wple_ref, wgate_ref, out_ref, hn_scr, acc_scr):
    j = pl.program_id(1)

    @pl.when(j == 0)
    def _():
        hn_scr[...] = _rms(x_ref[...], g_ref[...]).astype(BF16)
        acc_scr[...] = x_ref[...]

    h = jnp.maximum(_dot(hn_scr[...], wup_ref[...]), 0.0)
    acc_scr[...] += _dot((h * h).astype(BF16), wdn_ref[...])

    @pl.when(j == pl.num_programs(1) - 1)
    def _():
        x2 = acc_scr[...]
        hn2 = _rms(x2, g2_ref[...]).astype(BF16)
        gate = jax.nn.sigmoid(_dot(hn2, wgate_ref[...]))
        ple = _dot(p_ref[...].astype(BF16), wple_ref[...])
        out_ref[...] = x2 + ple * gate


def _mlp(x2d, gain, w_up, w_down, p2d, gain2, w_ple, w_gate):
    T, D = x2d.shape
    FF = w_up.shape[1]
    P = p2d.shape[1]
    tm = min(T, 1024)
    tf = min(FF, 1024)
    xspec = pl.BlockSpec((tm, D), lambda i, j: (i, 0))
    const = lambda shape: pl.BlockSpec(shape, lambda i, j: (0, 0))
    return pl.pallas_call(
        _mlp_kernel,
        grid=(T // tm, FF // tf),
        in_specs=[xspec, const((1, D)),
                  pl.BlockSpec((D, tf), lambda i, j: (0, j)),
                  pl.BlockSpec((tf, D), lambda i, j: (j, 0)),
                  pl.BlockSpec((tm, P), lambda i, j: (i, 0)),
                  const((1, D)), const((P, D)), const((D, D))],
        out_specs=xspec,
        out_shape=jax.ShapeDtypeStruct((T, D), F32),
        scratch_shapes=[pltpu.VMEM((tm, D), BF16), pltpu.VMEM((tm, D), F32)],
        compiler_params=_params(("parallel", "arbitrary")),
        name="channel_mixer",
    )(x2d, gain, w_up, w_down, p2d, gain2, w_ple, w_gate)


def _seg_cumsum(x, axis, reverse=False):
    n = x.shape[axis]
    pos = lax.broadcasted_iota(jnp.int32, x.shape, axis) & (CHUNK - 1)
    step = 1
    if reverse:
        acc = jnp.where(pos < CHUNK - 1, pltpu.roll(x, n - 1, axis), 0.0)
        x = acc
        while step < CHUNK:
            x = x + jnp.where(pos < CHUNK - step, pltpu.roll(x, n - step, axis), 0.0)
            step *= 2
        return x
    while step < CHUNK:
        x = x + jnp.where(pos >= step, pltpu.roll(x, step, axis), 0.0)
        step *= 2
    return x


def _softplus(x):
    return jnp.maximum(x, 0.0) + jnp.log1p(jnp.exp(-jnp.abs(x)))


def _dn_proj_kernel(x_ref, g_ref, wqkv_ref, wz_ref, wabt_ref, conv_ref, alog_c, dtb_c,
                    k_ref, kb_ref, vb_ref, kbg_ref, q_ref, qd_ref, kd_ref, z_ref, gcol_ref, grow_ref,
                    y_scr, *, tm):
    s = pl.program_id(1)
    H, E = DN_HEADS, DN_HEAD_DIM
    PAD = V7X_SUBLANES

    @pl.when(s == 0)
    def _():
        y_scr[0:PAD] = jnp.zeros((PAD, 3 * DN_WIDTH), F32)

    hn = _rms(x_ref[0], g_ref[...]).astype(BF16)
    y_scr[PAD:] = _dot(hn, wqkv_ref[...])
    z_ref[0] = _dot(hn, wz_ref[...]).astype(BF16)
    abt = _dot_nt(wabt_ref[...], hn)

    g_row = -jnp.exp(alog_c[...]) * _softplus(abt[:H] + dtb_c[...])
    beta_row = jax.nn.sigmoid(abt[H:])
    gam_row = _seg_cumsum(g_row, 1)
    suf_row = _seg_cumsum(g_row, 1, reverse=True)
    grow_ref[0] = gam_row
    rows = jnp.concatenate([gam_row, beta_row, jnp.exp(gam_row), jnp.exp(suf_row),
                            jnp.zeros((V7X_LANES - 4 * H, tm), F32)], axis=0)
    cols_t = rows.T
    gcol_ref[0] = cols_t

    def conv_silu(cb):
        cols = slice(cb * E, (cb + 1) * E)
        ext = y_scr[:, cols]
        y = conv_ref[CONV_WIDTH - 1:CONV_WIDTH, cols] * ext[PAD:]
        for back in range(1, CONV_WIDTH):
            tap = conv_ref[CONV_WIDTH - 1 - back:CONV_WIDTH - back, cols]
            y = y + tap * pltpu.roll(ext, back, 0)[PAD:]
        return y * jax.nn.sigmoid(y)

    def l2n(y):
        return y * lax.rsqrt(jnp.sum(y * y, axis=-1, keepdims=True) + EPS)

    for h in range(H):
        cols = slice(h * E, (h + 1) * E)
        q = l2n(conv_silu(h)) * (E ** -0.5)
        k = l2n(conv_silu(H + h))
        v = conv_silu(2 * H + h)
        beta = cols_t[:, H + h:H + h + 1]
        eg = cols_t[:, 2 * H + h:2 * H + h + 1]
        kb = k * beta
        k_ref[0, :, cols] = k.astype(BF16)
        kb_ref[0, :, cols] = kb.astype(BF16)
        vb_ref[0, :, cols] = (v * beta).astype(BF16)
        kbg_ref[0, :, cols] = (kb * eg).astype(BF16)
        q_ref[0, :, cols] = q.astype(BF16)
        qd_ref[0, :, cols] = (q * eg).astype(BF16)
        kd_ref[0, :, cols] = (k * cols_t[:, 3 * H + h:3 * H + h + 1]).astype(BF16)

    y_scr[0:PAD] = y_scr[tm:tm + PAD]


def _dn_proj(x, gain, w_qkv, w_z, w_abt, conv_w, alog_c, dtb_c):
    B, S, D = x.shape
    tm = min(S, 256)
    W = DN_WIDTH
    big = jax.ShapeDtypeStruct((B, S, W), BF16)
    bspec = pl.BlockSpec((1, tm, W), lambda b, s: (b, s, 0))
    const = lambda shape: pl.BlockSpec(shape, lambda b, s: (0,) * len(shape))
    return pl.pallas_call(
        functools.partial(_dn_proj_kernel, tm=tm),
        grid=(B, S // tm),
        in_specs=[pl.BlockSpec((1, tm, D), lambda b, s: (b, s, 0)), const((1, D)),
                  const((D, 3 * W)), const((D, W)), const((2 * DN_HEADS, D)),
                  const((CONV_WIDTH, 3 * W)), const((DN_HEADS, 1)), const((DN_HEADS, 1))],
        out_specs=[bspec] * 8 + [pl.BlockSpec((1, tm, V7X_LANES), lambda b, s: (b, s, 0)),
                                 pl.BlockSpec((1, DN_HEADS, tm), lambda b, s: (b, 0, s))],
        out_shape=[big] * 8 + [jax.ShapeDtypeStruct((B, S, V7X_LANES), F32),
                               jax.ShapeDtypeStruct((B, DN_HEADS, S), F32)],
        scratch_shapes=[pltpu.VMEM((tm + V7X_SUBLANES, 3 * W), F32)],
        compiler_params=_params(("parallel", "arbitrary")),
        name="deltanet_proj",
    )(x, gain, w_qkv, w_z, w_abt, conv_w, alog_c, dtb_c)


def _delta_kernel(k_ref, kb_ref, vb_ref, kbg_ref, q_ref, qd_ref, kd_ref, z_ref, gcol_ref, grow_ref, og_ref,
                  out_ref, state, *, tc, nh):
    hb = pl.program_id(1)
    n = pl.program_id(2)
    C, E = CHUNK, DN_HEAD_DIM
    nc = tc // C

    @pl.when(n == 0)
    def _():
        state[...] = jnp.zeros_like(state)

    lane = lax.broadcasted_iota(jnp.int32, (tc, V7X_LANES), 1)
    ri = lax.broadcasted_iota(jnp.int32, (C, C), 0)
    ci = lax.broadcasted_iota(jnp.int32, (C, C), 1)
    lower = ri >= ci
    strict = ri > ci
    gam_c, gam_r = [], []
    for i in range(nh):
        h = hb * nh + i
        gam_c.append(jnp.sum(jnp.where(lane == h, gcol_ref[0], 0.0), axis=-1, keepdims=True))
        gam_r.append(grow_ref[0, pl.ds(h, 1), :])

    units = [(i, c) for i in range(nh) for c in range(nc)]
    blk = lambda ref, i, c: ref[0, c * C:(c + 1) * C, i * E:(i + 1) * E]

    attn, pw = [], []
    for i, c in units:
        rows = slice(c * C, (c + 1) * C)
        diff = gam_c[i][rows] - gam_r[i][:, rows]
        decay = jnp.exp(jnp.where(lower, diff, NEG_BIG))
        gram = _dot_nt(jnp.concatenate([blk(q_ref, i, c), blk(kb_ref, i, c)], axis=0), blk(k_ref, i, c))
        attn.append((gram[:C] * decay).astype(BF16))
        pw.append(-jnp.where(strict, gram[C:] * decay, 0.0))
    nm = list(pw)
    for _ in range(int(math.log2(C)) - 1):
        pwb = [x.astype(BF16) for x in pw]
        pw = [_dot(x, x) for x in pwb]
        pwb = [x.astype(BF16) for x in pw]
        nm = [a + b + _dot(a.astype(BF16), bb) for a, b, bb in zip(nm, pw, pwb)]
    u, w = [], []
    for (i, c), a in zip(units, nm):
        rhs = jnp.concatenate([blk(vb_ref, i, c), blk(kbg_ref, i, c)], axis=1)
        sol = rhs.astype(F32) + _dot(a.astype(BF16), rhs)
        u.append(sol[:, :E])
        w.append(sol[:, E:].astype(BF16))

    S = [state[i] for i in range(nh)]
    for c in range(nc):
        rows = slice(c * C, (c + 1) * C)
        ws = [_dot(jnp.concatenate([w[i * nc + c], blk(qd_ref, i, c)], axis=0), S[i].astype(BF16))
              for i in range(nh)]
        vnb = [(u[i * nc + c] - ws[i][:C]).astype(BF16) for i in range(nh)]
        o = [ws[i][C:] + _dot(attn[i * nc + c], vnb[i]) for i in range(nh)]
        S = [S[i] * jnp.exp(gam_c[i][c * C + C - 1:c * C + C, :]) + _dot_tn(blk(kd_ref, i, c), vnb[i])
             for i in range(nh)]
        for i in range(nh):
            zc = blk(z_ref, i, c).astype(F32)
            out_ref[0, rows, i * E:(i + 1) * E] = (
                _rms(o[i], og_ref[...]) * (zc * jax.nn.sigmoid(zc))).astype(BF16)
    for i in range(nh):
        state[i] = S[i]


def _delta_rule(k, kb, vb, kbg, q, qd, kd, z, gcol, grow, o_gain):
    B, S, W = k.shape
    E = DN_HEAD_DIM
    tc = min(S, 512)
    nh = 4
    hspec = pl.BlockSpec((1, tc, nh * E), lambda b, h, n: (b, n, h))
    return pl.pallas_call(
        functools.partial(_delta_kernel, tc=tc, nh=nh),
        grid=(B, DN_HEADS // nh, S // tc),
        in_specs=[hspec] * 8 + [pl.BlockSpec((1, tc, V7X_LANES), lambda b, h, n: (b, n, 0)),
                                pl.BlockSpec((1, DN_HEADS, tc), lambda b, h, n: (b, 0, n)),
                                pl.BlockSpec((1, E), lambda b, h, n: (0, 0))],
        out_specs=hspec,
        out_shape=jax.ShapeDtypeStruct((B, S, W), BF16),
        scratch_shapes=[pltpu.VMEM((nh, E, E), F32)],
        compiler_params=_params(("parallel", "parallel", "arbitrary")),
        name="delta_rule",
    )(k, kb, vb, kbg, q, qd, kd, z, gcol, grow, o_gain)


def _attention_layer(x, tables, mix_gain, w_qkv, q_gain, k_gain, w_o):
    B, S, D = x.shape
    gain = mix_gain[None, :]
    bd = jnp.kron(jnp.eye(A_HEADS, dtype=F32), jnp.ones((A_HEAD_DIM, A_HEAD_DIM), F32)).astype(BF16)
    assert all(window // dilation == BAND for window, dilation in SWA_GROUPS)
    qkv = _qkv_all(x, gain, w_qkv.astype(BF16), jnp.tile(q_gain, (1, A_HEADS)), jnp.tile(k_gain, (1, A_HEADS)),
                   tables, bd)
    os, ls = [], []
    for (q, k, v), (_, dilation) in zip(qkv, SWA_GROUPS):
        o, lse = _attn_group(q, k, v, dilation)
        os.append(o)
        ls.append(lse)
    return _merge_out(os, ls, x, w_o.astype(BF16))


def _deltanet_layer(x, mix_gain, w_in, conv_w, a_log, dt_bias, o_gain, w_o):
    B, S, D = x.shape
    W, H = DN_WIDTH, DN_HEADS
    c = 3 * W
    w_qkv = w_in[:, :c].astype(BF16)
    w_a = w_in[:, c:c + H]
    w_b = w_in[:, c + H:c + 2 * H]
    w_z = w_in[:, c + 2 * H:].astype(BF16)
    w_abt = jnp.concatenate([w_a, w_b], axis=1).T.astype(BF16)
    outs = _dn_proj(x, mix_gain[None, :], w_qkv, w_z, w_abt, conv_w, a_log[:, None], dt_bias[:, None])
    o = _delta_rule(*outs, o_gain[None, :])
    return _proj_residual(o.reshape(B * S, W), x.reshape(B * S, D), w_o.astype(BF16)).reshape(B, S, D)


def kernel(x, p, positions, mix_norm, attn_w_qkv, attn_q_gain, attn_k_gain, attn_w_o, dn_w_in, dn_conv,
           dn_a_log, dn_dt_bias, dn_o_gain, dn_w_o, mlp_norm, w_up, w_down, ple_norm, w_ple, w_ple_gate):
    B, S, D = x.shape
    depth = p.shape[0]
    tables = _rope_tables(positions)
    tables = [t.reshape(B, S, V7X_LANES) for t in tables]
    for i in range(depth):
        j = i // 2
        if i % 2 == 0:
            x = _attention_layer(x, tables, mix_norm[i], attn_w_qkv[j], attn_q_gain[j], attn_k_gain[j],
                                 attn_w_o[j])
        else:
            x = _deltanet_layer(x, mix_norm[i], dn_w_in[j], dn_conv[j], dn_a_log[j], dn_dt_bias[j],
                                dn_o_gain[j], dn_w_o[j])
        x = _mlp(x.reshape(B * S, D), mlp_norm[i][None, :], w_up[i].astype(BF16), w_down[i].astype(BF16),
                 p[i].reshape(B * S, -1), ple_norm[i][None, :], w_ple[i].astype(BF16),
                 w_ple_gate[i].astype(BF16)).reshape(B, S, D)
    return x
```

```python
import functools
import math

import jax
import jax.numpy as jnp
from jax import lax
from jax.experimental import pallas as pl
from jax.experimental.pallas import tpu as pltpu

F32 = jnp.float32
BF16 = jnp.bfloat16

EPS = 1e-6
SWA_GROUPS = ((128, 1), (512, 4), (2048, 16))
A_HEADS = 8
A_HEAD_DIM = 64
A_WIDTH = A_HEADS * A_HEAD_DIM
ROPE_DIM = A_HEAD_DIM // 4
ROPE_HALF = ROPE_DIM // 2
ROPE_THETA = 500000.0
BAND = 128
DN_HEADS = 8
DN_HEAD_DIM = 128
DN_WIDTH = DN_HEADS * DN_HEAD_DIM
CONV_WIDTH = 4
CHUNK = 64
NEG_BIG = -1e30

V7X_LANES = 128
V7X_SUBLANES = 8
V7X_MXU_DIM = 256
V7X_VMEM_LIMIT = 56 * 1024 * 1024


def _params(semantics):
    return pltpu.CompilerParams(dimension_semantics=semantics, vmem_limit_bytes=V7X_VMEM_LIMIT)


def _rms(x, gain):
    ms = jnp.mean(x * x, axis=-1, keepdims=True)
    return x * lax.rsqrt(ms + EPS) * gain


def _dot(a, b):
    return jnp.dot(a, b, preferred_element_type=F32)


def _dot_nt(a, b):
    return lax.dot_general(a, b, (((1,), (1,)), ((), ())), preferred_element_type=F32)


def _dot_tn(a, b):
    return lax.dot_general(a, b, (((0,), (0,)), ((), ())), preferred_element_type=F32)


def _rope_tables_kernel(pos_ref, invf_ref, c_ref, s1_ref, s2_ref):
    ang = pos_ref[...].astype(F32) * invf_ref[...]
    e = lax.broadcasted_iota(jnp.int32, ang.shape, 1) & (A_HEAD_DIM - 1)
    cos = jnp.cos(ang)
    sin = jnp.sin(ang)
    c_ref[...] = jnp.where(e < ROPE_DIM, cos, 1.0)
    s1_ref[...] = jnp.where(e < ROPE_HALF, -sin, 0.0)
    s2_ref[...] = jnp.where(e < ROPE_HALF, 0.0, jnp.where(e < ROPE_DIM, sin, 0.0))


def _rope_tables(positions):
    T = positions.size
    tm = min(T, 2048)
    inv_freq = ROPE_THETA ** (-jnp.arange(0, ROPE_DIM, 2, dtype=F32) / ROPE_DIM)
    invf = jnp.tile(inv_freq, V7X_LANES // ROPE_HALF)[None, :]
    out = jax.ShapeDtypeStruct((T, V7X_LANES), F32)
    spec = pl.BlockSpec((tm, V7X_LANES), lambda i: (i, 0))
    return pl.pallas_call(
        _rope_tables_kernel,
        grid=(T // tm,),
        in_specs=[pl.BlockSpec((tm, 1), lambda i: (i, 0)),
                  pl.BlockSpec((1, V7X_LANES), lambda i: (0, 0))],
        out_specs=[spec, spec, spec],
        out_shape=[out, out, out],
        compiler_params=_params(("parallel",)),
        name="rope_tables",
    )(positions.reshape(T, 1), invf)


def _by_residue(ref, tm, d):
    if d == 1:
        return ref[0]
    return jnp.concatenate([ref[0, pl.ds(r, tm // d, stride=d), :] for r in range(d)], axis=0)


def _qkv_kernel(*refs, tm, ncol):
    x_refs = refs[:ncol]
    g_ref, w_ref, qg_ref, kg_ref, c_ref, s1_ref, s2_ref, bd_ref = refs[ncol:ncol + 8]
    out_refs = refs[ncol + 8:]
    reps = A_WIDTH // V7X_LANES
    bdw = bd_ref.shape[0]
    for gi, (_, d) in enumerate(SWA_GROUPS):
        q_ref, k_ref, v_ref = out_refs[3 * gi:3 * gi + 3]
        xg = jnp.concatenate([_by_residue(xr, tm, d) for xr in x_refs], axis=1)
        hn = _rms(xg, g_ref[...]).astype(BF16)
        acc = _dot(hn, w_ref[:, 3 * A_WIDTH * gi:3 * A_WIDTH * (gi + 1)])
        c = jnp.concatenate([_by_residue(c_ref, tm, d)] * reps, axis=1)
        s1 = jnp.concatenate([_by_residue(s1_ref, tm, d)] * reps, axis=1)
        s2 = jnp.concatenate([_by_residue(s2_ref, tm, d)] * reps, axis=1)

        def norm_rope(y, gain):
            y2 = (y * y).astype(BF16)
            ms = jnp.concatenate([_dot(y2[:, i:i + bdw], bd_ref[...]) for i in range(0, A_WIDTH, bdw)], axis=1)
            yn = y * lax.rsqrt(ms + EPS) * gain
            hi = pltpu.roll(yn, A_WIDTH - ROPE_HALF, 1)
            lo = pltpu.roll(yn, ROPE_HALF, 1)
            return yn * c + hi * s1 + lo * s2

        q = norm_rope(acc[:, :A_WIDTH], qg_ref[gi:gi + 1, :]).astype(BF16)
        k = norm_rope(acc[:, A_WIDTH:2 * A_WIDTH], kg_ref[gi:gi + 1, :]).astype(BF16)
        v = acc[:, 2 * A_WIDTH:].astype(BF16)
        n = tm // d
        for r in range(d):
            q_ref[0, r] = q[r * n:(r + 1) * n]
            k_ref[0, r] = k[r * n:(r + 1) * n]
            v_ref[0, r] = v[r * n:(r + 1) * n]


def _qkv_all(x, gain, w_qkv, q_gain, k_gain, tables, bd):
    B, S, D = x.shape
    tm = min(S, 512)
    G = len(SWA_GROUPS)
    tspec = pl.BlockSpec((1, tm, V7X_LANES), lambda b, n: (b, n, 0))
    const = lambda shape: pl.BlockSpec(shape, lambda b, n: (0,) * len(shape))
    out_specs, out_shape = [], []
    for _, d in SWA_GROUPS:
        out_specs += [pl.BlockSpec((1, d, tm // d, A_WIDTH), lambda b, n: (b, 0, n, 0))] * 3
        out_shape += [jax.ShapeDtypeStruct((B, d, S // d, A_WIDTH), BF16)] * 3
    ncol = D // V7X_LANES
    xspecs = [pl.BlockSpec((1, tm, V7X_LANES), functools.partial(lambda b, n, c: (b, n, c), c=c))
              for c in range(ncol)]
    outs = pl.pallas_call(
        functools.partial(_qkv_kernel, tm=tm, ncol=ncol),
        grid=(B, S // tm),
        in_specs=xspecs + [const((1, D)), const((D, G * 3 * A_WIDTH)), const((G, A_WIDTH)), const((G, A_WIDTH)),
                           tspec, tspec, tspec, const(bd.shape)],
        out_specs=out_specs,
        out_shape=out_shape,
        compiler_params=_params(("parallel", "parallel")),
        name="attn_qkv",
    )(*([x] * ncol), gain, w_qkv, q_gain, k_gain, *tables, bd)
    return [outs[3 * gi:3 * gi + 3] for gi in range(G)]


def _attn_kernel(q_ref, kc_ref, kp_ref, vc_ref, vp_ref, o_ref, lse_ref, kbuf, vbuf, *, tq):
    n = pl.program_id(2)
    kbuf[0:BAND] = kp_ref[0, 0]
    kbuf[BAND:] = kc_ref[0, 0]
    vbuf[0:BAND] = vp_ref[0, 0]
    vbuf[BAND:] = vc_ref[0, 0]

    qi = lax.broadcasted_iota(jnp.int32, (BAND, 2 * BAND), 0)
    kj = lax.broadcasted_iota(jnp.int32, (BAND, 2 * BAND), 1)
    band = jnp.where(kj >= qi, jnp.where(kj <= qi + BAND, 0.0, NEG_BIG), NEG_BIG)
    first_lo = jnp.where(n > 0, 0, BAND)
    band_first = jnp.where(kj >= first_lo, band, NEG_BIG)
    lane = lax.broadcasted_iota(jnp.int32, (BAND, V7X_LANES), 1)
    even = lane < A_HEAD_DIM
    lane_row = lax.broadcasted_iota(jnp.int32, (1, V7X_LANES), 1)
    head_masks = (jnp.where(lane_row < A_HEAD_DIM, 1.0, 0.0).astype(BF16),
                  jnp.where(lane_row < A_HEAD_DIM, 0.0, 1.0).astype(BF16))

    for j in range(tq // BAND):
        bias = band_first if j == 0 else band
        rows = slice(j * BAND, (j + 1) * BAND)
        for p in range(A_WIDTH // V7X_LANES):
            cols = slice(p * V7X_LANES, (p + 1) * V7X_LANES)
            qp = q_ref[0, 0, rows, cols]
            kk = kbuf[j * BAND:(j + 2) * BAND, cols]
            vv = vbuf[j * BAND:(j + 2) * BAND, cols]
            outs, lses = [], []
            for hm in head_masks:
                s = _dot_nt(qp * hm, kk) + bias
                m = jnp.max(s, axis=-1, keepdims=True)
                e = jnp.exp(s - m)
                l = jnp.sum(e, axis=-1, keepdims=True)
                o = _dot(e.astype(BF16), vv)
                outs.append(o * (1.0 / l))
                lses.append(m + jnp.log(l))
            o_ref[0, 0, rows, cols] = jnp.where(even, outs[0], outs[1]).astype(BF16)
            lse_ref[0, 0, rows, cols] = jnp.where(even, lses[0], lses[1])


def _attn_group(q, k, v, dilation):
    B, d, L, W = q.shape
    tq = min(L, 512)
    nsub = tq // BAND
    cur = pl.BlockSpec((1, 1, tq, W), lambda b, r, n: (b, r, n, 0))
    prev = pl.BlockSpec((1, 1, BAND, W), lambda b, r, n: (b, r, jnp.maximum(n * nsub - 1, 0), 0))
    return pl.pallas_call(
        functools.partial(_attn_kernel, tq=tq),
        grid=(B, d, L // tq),
        in_specs=[cur, cur, prev, cur, prev],
        out_specs=[cur, cur],
        out_shape=[jax.ShapeDtypeStruct((B, d, L, W), BF16),
                   jax.ShapeDtypeStruct((B, d, L, W), F32)],
        scratch_shapes=[pltpu.VMEM((BAND + tq, W), BF16), pltpu.VMEM((BAND + tq, W), BF16)],
        compiler_params=_params(("parallel", "parallel", "arbitrary")),
        name=f"band_attn_d{dilation}",
    )(q, k, k, v, v)


def _merge_out_kernel(*refs, tm):
    G = len(SWA_GROUPS)
    o_refs, l_refs = refs[:G], refs[G:2 * G]
    x_ref, w_ref, out_ref, o_scr, l_scr = refs[2 * G:]
    os, ls = [], []
    for gi, (_, d) in enumerate(SWA_GROUPS):
        if d == 1:
            os.append(o_refs[gi][0, 0].astype(F32))
            ls.append(l_refs[gi][0, 0])
            continue
        ncol = A_WIDTH // V7X_LANES
        for r in range(d):
            ov = o_refs[gi][0, r].astype(F32)
            lv = l_refs[gi][0, r]
            for c in range(ncol):
                cols = slice(c * V7X_LANES, (c + 1) * V7X_LANES)
                o_scr[gi * ncol + c, pl.ds(r, tm // d, stride=d), :] = ov[:, cols]
                l_scr[gi * ncol + c, pl.ds(r, tm // d, stride=d), :] = lv[:, cols]
        os.append(jnp.concatenate([o_scr[gi * ncol + c] for c in range(ncol)], axis=1))
        ls.append(jnp.concatenate([l_scr[gi * ncol + c] for c in range(ncol)], axis=1))
    m = functools.reduce(jnp.maximum, ls)
    es = [jnp.exp(a - m) for a in ls]
    inv = 1.0 / functools.reduce(lambda a, b: a + b, es)
    merged = functools.reduce(lambda a, b: a + b, [(e * inv) * o for e, o in zip(es, os)])
    out_ref[0] = x_ref[0] + _dot(merged.astype(BF16), w_ref[...])


def _merge_out(os, ls, x, w_o):
    B, S, D = x.shape
    tm = min(S, 512)
    G = len(SWA_GROUPS)
    gspecs = [pl.BlockSpec((1, d, tm // d, A_WIDTH), lambda b, n: (b, 0, n, 0)) for _, d in SWA_GROUPS]
    xspec = pl.BlockSpec((1, tm, D), lambda b, n: (b, n, 0))
    return pl.pallas_call(
        functools.partial(_merge_out_kernel, tm=tm),
        grid=(B, S // tm),
        in_specs=gspecs + gspecs + [xspec, pl.BlockSpec((A_WIDTH, D), lambda b, n: (0, 0))],
        out_specs=xspec,
        out_shape=jax.ShapeDtypeStruct((B, S, D), F32),
        scratch_shapes=[pltpu.VMEM((G * A_WIDTH // V7X_LANES, tm, V7X_LANES), F32)] * 2,
        compiler_params=_params(("parallel", "parallel")),
        name="attn_merge_out",
    )(*os, *ls, x, w_o)


def _mlp_kernel(*refs, has_mix):
    if has_mix:
        a_ref, wmix_ref = refs[:2]
        refs = refs[2:]
    x_ref, g_ref, wup_ref, wdn_ref, p_ref, g2_ref, wple_ref, wgate_ref, out_ref, hn_scr, acc_scr = refs
    j = pl.program_id(1)

    @pl.when(j == 0)
    def _():
        x1 = x_ref[...]
        if has_mix:
            x1 = x1 + _dot(a_ref[...], wmix_ref[...])
        hn_scr[...] = _rms(x1, g_ref[...]).astype(BF16)
        acc_scr[...] = x1

    h = jnp.maximum(_dot(hn_scr[...], wup_ref[...]), 0.0)
    acc_scr[...] += _dot((h * h).astype(BF16), wdn_ref[...])

    @pl.when(j == pl.num_programs(1) - 1)
    def _():
        x2 = acc_scr[...]
        hn2 = _rms(x2, g2_ref[...]).astype(BF16)
        gate = jax.nn.sigmoid(_dot(hn2, wgate_ref[...]))
        ple = _dot(p_ref[...].astype(BF16), wple_ref[...])
        out_ref[...] = x2 + ple * gate


def _mlp(x2d, gain, w_up, w_down, p2d, gain2, w_ple, w_gate, mix=None, w_mix=None):
    T, D = x2d.shape
    FF = w_up.shape[1]
    P = p2d.shape[1]
    tm = min(T, 1024)
    tf = min(FF, 1024)
    xspec = pl.BlockSpec((tm, D), lambda i, j: (i, 0))
    const = lambda shape: pl.BlockSpec(shape, lambda i, j: (0, 0))
    has_mix = mix is not None
    mix_specs = [pl.BlockSpec((tm, mix.shape[1]), lambda i, j: (i, 0)), const(w_mix.shape)] if has_mix else []
    mix_args = [mix, w_mix] if has_mix else []
    return pl.pallas_call(
        functools.partial(_mlp_kernel, has_mix=has_mix),
        grid=(T // tm, FF // tf),
        in_specs=mix_specs + [xspec, const((1, D)),
                  pl.BlockSpec((D, tf), lambda i, j: (0, j)),
                  pl.BlockSpec((tf, D), lambda i, j: (j, 0)),
                  pl.BlockSpec((tm, P), lambda i, j: (i, 0)),
                  const((1, D)), const((P, D)), const((D, D))],
        out_specs=xspec,
        out_shape=jax.ShapeDtypeStruct((T, D), F32),
        scratch_shapes=[pltpu.VMEM((tm, D), BF16), pltpu.VMEM((tm, D), F32)],
        compiler_params=_params(("parallel", "arbitrary")),
        name="channel_mixer",
    )(*mix_args, x2d, gain, w_up, w_down, p2d, gain2, w_ple, w_gate)


def _seg_cumsum(x, axis, reverse=False):
    n = x.shape[axis]
    pos = lax.broadcasted_iota(jnp.int32, x.shape, axis) & (CHUNK - 1)
    step = 1
    if reverse:
        acc = jnp.where(pos < CHUNK - 1, pltpu.roll(x, n - 1, axis), 0.0)
        x = acc
        while step < CHUNK:
            x = x + jnp.where(pos < CHUNK - step, pltpu.roll(x, n - step, axis), 0.0)
            step *= 2
        return x
    while step < CHUNK:
        x = x + jnp.where(pos >= step, pltpu.roll(x, step, axis), 0.0)
        step *= 2
    return x


def _softplus(x):
    return jnp.maximum(x, 0.0) + jnp.log1p(jnp.exp(-jnp.abs(x)))


def _dn_proj_kernel(x_ref, g_ref, wqkv_ref, wz_ref, wabt_ref, conv_ref, alog_c, dtb_c,
                    k_ref, kb_ref, vb_ref, kbg_ref, q_ref, qd_ref, kd_ref, z_ref, gcol_ref, grow_ref,
                    y_scr, *, tm):
    s = pl.program_id(1)
    H, E = DN_HEADS, DN_HEAD_DIM
    PAD = V7X_SUBLANES

    @pl.when(s == 0)
    def _():
        y_scr[0:PAD] = jnp.zeros((PAD, 3 * DN_WIDTH), F32)

    hn = _rms(x_ref[0], g_ref[...]).astype(BF16)
    y_scr[PAD:] = _dot(hn, wqkv_ref[...])
    z_ref[0] = _dot(hn, wz_ref[...]).astype(BF16)
    abt = _dot_nt(wabt_ref[...], hn)

    g_row = -jnp.exp(alog_c[...]) * _softplus(abt[:H] + dtb_c[...])
    beta_row = jax.nn.sigmoid(abt[H:])
    gam_row = _seg_cumsum(g_row, 1)
    suf_row = _seg_cumsum(g_row, 1, reverse=True)
    grow_ref[0] = gam_row
    rows = jnp.concatenate([gam_row, beta_row, jnp.exp(gam_row), jnp.exp(suf_row),
                            jnp.zeros((V7X_LANES - 4 * H, tm), F32)], axis=0)
    cols_t = rows.T
    gcol_ref[0] = cols_t

    def conv_silu(cb):
        cols = slice(cb * E, (cb + 1) * E)
        ext = y_scr[:, cols]
        y = conv_ref[CONV_WIDTH - 1:CONV_WIDTH, cols] * ext[PAD:]
        for back in range(1, CONV_WIDTH):
            tap = conv_ref[CONV_WIDTH - 1 - back:CONV_WIDTH - back, cols]
            y = y + tap * pltpu.roll(ext, back, 0)[PAD:]
        return y * jax.nn.sigmoid(y)

    def l2n(y, scale=1.0):
        return y * (lax.rsqrt(jnp.sum(y * y, axis=-1, keepdims=True) + EPS) * scale)

    for h in range(H):
        cols = slice(h * E, (h + 1) * E)
        q = l2n(conv_silu(h), E ** -0.5)
        k = l2n(conv_silu(H + h))
        v = conv_silu(2 * H + h)
        beta = cols_t[:, H + h:H + h + 1]
        eg = cols_t[:, 2 * H + h:2 * H + h + 1]
        kb = k * beta
        k_ref[0, :, cols] = k.astype(BF16)
        kb_ref[0, :, cols] = kb.astype(BF16)
        vb_ref[0, :, cols] = (v * beta).astype(BF16)
        kbg_ref[0, :, cols] = (kb * eg).astype(BF16)
        q_ref[0, :, cols] = q.astype(BF16)
        qd_ref[0, :, cols] = (q * eg).astype(BF16)
        kd_ref[0, :, cols] = (k * cols_t[:, 3 * H + h:3 * H + h + 1]).astype(BF16)

    y_scr[0:PAD] = y_scr[tm:tm + PAD]


def _dn_proj(x, gain, w_qkv, w_z, w_abt, conv_w, alog_c, dtb_c):
    B, S, D = x.shape
    tm = min(S, 256)
    W = DN_WIDTH
    big = jax.ShapeDtypeStruct((B, S, W), BF16)
    bspec = pl.BlockSpec((1, tm, W), lambda b, s: (b, s, 0))
    const = lambda shape: pl.BlockSpec(shape, lambda b, s: (0,) * len(shape))
    return pl.pallas_call(
        functools.partial(_dn_proj_kernel, tm=tm),
        grid=(B, S // tm),
        in_specs=[pl.BlockSpec((1, tm, D), lambda b, s: (b, s, 0)), const((1, D)),
                  const((D, 3 * W)), const((D, W)), const((2 * DN_HEADS, D)),
                  const((CONV_WIDTH, 3 * W)), const((DN_HEADS, 1)), const((DN_HEADS, 1))],
        out_specs=[bspec] * 8 + [pl.BlockSpec((1, tm, V7X_LANES), lambda b, s: (b, s, 0)),
                                 pl.BlockSpec((1, DN_HEADS, tm), lambda b, s: (b, 0, s))],
        out_shape=[big] * 8 + [jax.ShapeDtypeStruct((B, S, V7X_LANES), F32),
                               jax.ShapeDtypeStruct((B, DN_HEADS, S), F32)],
        scratch_shapes=[pltpu.VMEM((tm + V7X_SUBLANES, 3 * W), F32)],
        compiler_params=_params(("parallel", "arbitrary")),
        name="deltanet_proj",
    )(x, gain, w_qkv, w_z, w_abt, conv_w, alog_c, dtb_c)


def _delta_kernel(k_ref, kb_ref, vb_ref, kbg_ref, q_ref, qd_ref, kd_ref, z_ref, gcol_ref, grow_ref, og_ref,
                  out_ref, state, *, tc, nh):
    hb = pl.program_id(1)
    n = pl.program_id(2)
    C, E = CHUNK, DN_HEAD_DIM
    nc = tc // C

    @pl.when(n == 0)
    def _():
        state[...] = jnp.zeros_like(state)

    lane = lax.broadcasted_iota(jnp.int32, (tc, V7X_LANES), 1)
    ri = lax.broadcasted_iota(jnp.int32, (C, C), 0)
    ci = lax.broadcasted_iota(jnp.int32, (C, C), 1)
    lower = ri >= ci
    strict = ri > ci
    gam_c, gam_r = [], []
    for i in range(nh):
        h = hb * nh + i
        gam_c.append(jnp.sum(jnp.where(lane == h, gcol_ref[0], 0.0), axis=-1, keepdims=True))
        gam_r.append(grow_ref[0, pl.ds(h, 1), :])

    units = [(i, c) for i in range(nh) for c in range(nc)]
    blk = lambda ref, i, c: ref[0, c * C:(c + 1) * C, i * E:(i + 1) * E]

    attn, pw = [], []
    for i, c in units:
        rows = slice(c * C, (c + 1) * C)
        diff = gam_c[i][rows] - gam_r[i][:, rows]
        decay = jnp.exp(jnp.where(lower, diff, NEG_BIG))
        gram = _dot_nt(jnp.concatenate([blk(q_ref, i, c), blk(kb_ref, i, c)], axis=0), blk(k_ref, i, c))
        attn.append((gram[:C] * decay).astype(BF16))
        pw.append(-jnp.where(strict, gram[C:] * decay, 0.0))
    nm = list(pw)
    for _ in range(int(math.log2(C)) - 1):
        pwb = [x.astype(BF16) for x in pw]
        pw = [_dot(x, x) for x in pwb]
        pwb = [x.astype(BF16) for x in pw]
        nm = [a + b + _dot(a.astype(BF16), bb) for a, b, bb in zip(nm, pw, pwb)]
    u, w = [], []
    for (i, c), a in zip(units, nm):
        rhs = jnp.concatenate([blk(vb_ref, i, c), blk(kbg_ref, i, c)], axis=1)
        sol = rhs.astype(F32) + _dot(a.astype(BF16), rhs)
        u.append(sol[:, :E])
        w.append(sol[:, E:].astype(BF16))

    S = [state[i] for i in range(nh)]
    for c in range(nc):
        rows = slice(c * C, (c + 1) * C)
        ws = [_dot(jnp.concatenate([w[i * nc + c], blk(qd_ref, i, c)], axis=0), S[i].astype(BF16))
              for i in range(nh)]
        vnb = [(u[i * nc + c] - ws[i][:C]).astype(BF16) for i in range(nh)]
        o = [ws[i][C:] + _dot(attn[i * nc + c], vnb[i]) for i in range(nh)]
        S = [S[i] * jnp.exp(gam_c[i][c * C + C - 1:c * C + C, :]) + _dot_tn(blk(kd_ref, i, c), vnb[i])
             for i in range(nh)]
        for i in range(nh):
            zc = blk(z_ref, i, c).astype(F32)
            out_ref[0, rows, i * E:(i + 1) * E] = (
                _rms(o[i], og_ref[...]) * (zc * jax.nn.sigmoid(zc))).astype(BF16)
    for i in range(nh):
        state[i] = S[i]


def _delta_rule(k, kb, vb, kbg, q, qd, kd, z, gcol, grow, o_gain):
    B, S, W = k.shape
    E = DN_HEAD_DIM
    tc = min(S, 256)
    nh = DN_HEADS
    hspec = pl.BlockSpec((1, tc, nh * E), lambda b, h, n: (b, n, h))
    return pl.pallas_call(
        functools.partial(_delta_kernel, tc=tc, nh=nh),
        grid=(B, DN_HEADS // nh, S // tc),
        in_specs=[hspec] * 8 + [pl.BlockSpec((1, tc, V7X_LANES), lambda b, h, n: (b, n, 0)),
                                pl.BlockSpec((1, DN_HEADS, tc), lambda b, h, n: (b, 0, n)),
                                pl.BlockSpec((1, E), lambda b, h, n: (0, 0))],
        out_specs=hspec,
        out_shape=jax.ShapeDtypeStruct((B, S, W), BF16),
        scratch_shapes=[pltpu.VMEM((nh, E, E), F32)],
        compiler_params=_params(("parallel", "parallel", "arbitrary")),
        name="delta_rule",
    )(k, kb, vb, kbg, q, qd, kd, z, gcol, grow, o_gain)


def _attention_layer(x, tables, mix_gain, w_qkv, q_gain, k_gain, w_o):
    B, S, D = x.shape
    gain = mix_gain[None, :]
    heads_per_pass = V7X_MXU_DIM // A_HEAD_DIM
    bd = jnp.kron(jnp.eye(heads_per_pass, dtype=F32),
                  jnp.full((A_HEAD_DIM, A_HEAD_DIM), 1.0 / A_HEAD_DIM, F32)).astype(BF16)
    q_gain = q_gain * (A_HEAD_DIM ** -0.5)
    assert all(window // dilation == BAND for window, dilation in SWA_GROUPS)
    qkv = _qkv_all(x, gain, w_qkv.astype(BF16), jnp.tile(q_gain, (1, A_HEADS)), jnp.tile(k_gain, (1, A_HEADS)),
                   tables, bd)
    os, ls = [], []
    for (q, k, v), (_, dilation) in zip(qkv, SWA_GROUPS):
        o, lse = _attn_group(q, k, v, dilation)
        os.append(o)
        ls.append(lse)
    return _merge_out(os, ls, x, w_o.astype(BF16))


def _deltanet_layer(x, mix_gain, w_in, conv_w, a_log, dt_bias, o_gain):
    B, S, D = x.shape
    W, H = DN_WIDTH, DN_HEADS
    c = 3 * W
    w_qkv = w_in[:, :c].astype(BF16)
    w_a = w_in[:, c:c + H]
    w_b = w_in[:, c + H:c + 2 * H]
    w_z = w_in[:, c + 2 * H:].astype(BF16)
    w_abt = jnp.concatenate([w_a, w_b], axis=1).T.astype(BF16)
    outs = _dn_proj(x, mix_gain[None, :], w_qkv, w_z, w_abt, conv_w, a_log[:, None], dt_bias[:, None])
    return _delta_rule(*outs, o_gain[None, :]).reshape(B * S, W)


def kernel(x, p, positions, mix_norm, attn_w_qkv, attn_q_gain, attn_k_gain, attn_w_o, dn_w_in, dn_conv,
           dn_a_log, dn_dt_bias, dn_o_gain, dn_w_o, mlp_norm, w_up, w_down, ple_norm, w_ple, w_ple_gate):
    B, S, D = x.shape
    depth = p.shape[0]
    tables = _rope_tables(positions)
    tables = [t.reshape(B, S, V7X_LANES) for t in tables]
    for i in range(depth):
        j = i // 2
        mix, w_mix = None, None
        if i % 2 == 0:
            x = _attention_layer(x, tables, mix_norm[i], attn_w_qkv[j], attn_q_gain[j], attn_k_gain[j],
                                 attn_w_o[j])
        else:
            mix = _deltanet_layer(x, mix_norm[i], dn_w_in[j], dn_conv[j], dn_a_log[j], dn_dt_bias[j],
                                  dn_o_gain[j])
            w_mix = dn_w_o[j].astype(BF16)
        x = _mlp(x.reshape(B * S, D), mlp_norm[i][None, :], w_up[i].astype(BF16), w_down[i].astype(BF16),
                 p[i].reshape(B * S, -1), ple_norm[i][None, :], w_ple[i].astype(BF16),
                 w_ple_gate[i].astype(BF16), mix=mix, w_mix=w_mix).reshape(B, S, D)
    return x
```

```python
import functools
import math

import jax
import jax.numpy as jnp
from jax import lax
from jax.experimental import pallas as pl
from jax.experimental.pallas import tpu as pltpu

F32 = jnp.float32
BF16 = jnp.bfloat16

EPS = 1e-6
SWA_GROUPS = ((128, 1), (512, 4), (2048, 16))
A_HEADS = 8
A_HEAD_DIM = 64
A_WIDTH = A_HEADS * A_HEAD_DIM
ROPE_DIM = A_HEAD_DIM // 4
ROPE_HALF = ROPE_DIM // 2
ROPE_THETA = 500000.0
BAND = 128
DN_HEADS = 8
DN_HEAD_DIM = 128
DN_WIDTH = DN_HEADS * DN_HEAD_DIM
CONV_WIDTH = 4
CHUNK = 64
NEG_BIG = -1e30

V7X_LANES = 128
V7X_SUBLANES = 8
V7X_MXU_DIM = 256
V7X_VMEM_LIMIT = 56 * 1024 * 1024


def _params(semantics):
    return pltpu.CompilerParams(dimension_semantics=semantics, vmem_limit_bytes=V7X_VMEM_LIMIT)


def _rms(x, gain):
    ms = jnp.mean(x * x, axis=-1, keepdims=True)
    return x * lax.rsqrt(ms + EPS) * gain


def _dot(a, b):
    return jnp.dot(a, b, preferred_element_type=F32)


def _dot_nt(a, b):
    return lax.dot_general(a, b, (((1,), (1,)), ((), ())), preferred_element_type=F32)


def _dot_tn(a, b):
    return lax.dot_general(a, b, (((0,), (0,)), ((), ())), preferred_element_type=F32)


def _rope_tables_kernel(pos_ref, invf_ref, c_ref, s1_ref, s2_ref):
    ang = pos_ref[...].astype(F32) * invf_ref[...]
    e = lax.broadcasted_iota(jnp.int32, ang.shape, 1) & (A_HEAD_DIM - 1)
    cos = jnp.cos(ang)
    sin = jnp.sin(ang)
    c_ref[...] = jnp.where(e < ROPE_DIM, cos, 1.0)
    s1_ref[...] = jnp.where(e < ROPE_HALF, -sin, 0.0)
    s2_ref[...] = jnp.where(e < ROPE_HALF, 0.0, jnp.where(e < ROPE_DIM, sin, 0.0))


def _rope_tables(positions):
    T = positions.size
    tm = min(T, 2048)
    inv_freq = ROPE_THETA ** (-jnp.arange(0, ROPE_DIM, 2, dtype=F32) / ROPE_DIM)
    invf = jnp.tile(inv_freq, V7X_LANES // ROPE_HALF)[None, :]
    out = jax.ShapeDtypeStruct((T, V7X_LANES), F32)
    spec = pl.BlockSpec((tm, V7X_LANES), lambda i: (i, 0))
    return pl.pallas_call(
        _rope_tables_kernel,
        grid=(T // tm,),
        in_specs=[pl.BlockSpec((tm, 1), lambda i: (i, 0)),
                  pl.BlockSpec((1, V7X_LANES), lambda i: (0, 0))],
        out_specs=[spec, spec, spec],
        out_shape=[out, out, out],
        compiler_params=_params(("parallel",)),
        name="rope_tables",
    )(positions.reshape(T, 1), invf)


def _by_residue(ref, tm, d):
    if d == 1:
        return ref[0]
    return jnp.concatenate([ref[0, pl.ds(r, tm // d, stride=d), :] for r in range(d)], axis=0)


def _qkv_kernel(*refs, tm, ncol):
    x_refs = refs[:ncol]
    g_ref, w_ref, qg_ref, kg_ref, c_ref, s1_ref, s2_ref, bd_ref = refs[ncol:ncol + 8]
    out_refs = refs[ncol + 8:]
    reps = A_WIDTH // V7X_LANES
    bdw = bd_ref.shape[0]
    for gi, (_, d) in enumerate(SWA_GROUPS):
        q_ref, k_ref, v_ref = out_refs[3 * gi:3 * gi + 3]
        xg = jnp.concatenate([_by_residue(xr, tm, d) for xr in x_refs], axis=1)
        hn = _rms(xg, g_ref[...]).astype(BF16)
        acc = _dot(hn, w_ref[:, 3 * A_WIDTH * gi:3 * A_WIDTH * (gi + 1)])
        c = jnp.concatenate([_by_residue(c_ref, tm, d)] * reps, axis=1)
        s1 = jnp.concatenate([_by_residue(s1_ref, tm, d)] * reps, axis=1)
        s2 = jnp.concatenate([_by_residue(s2_ref, tm, d)] * reps, axis=1)

        def norm_rope(y, gain):
            y2 = (y * y).astype(BF16)
            ms = jnp.concatenate([_dot(y2[:, i:i + bdw], bd_ref[...]) for i in range(0, A_WIDTH, bdw)], axis=1)
            yn = y * lax.rsqrt(ms + EPS) * gain
            hi = pltpu.roll(yn, A_WIDTH - ROPE_HALF, 1)
            lo = pltpu.roll(yn, ROPE_HALF, 1)
            return yn * c + hi * s1 + lo * s2

        q = norm_rope(acc[:, :A_WIDTH], qg_ref[gi:gi + 1, :]).astype(BF16)
        k = norm_rope(acc[:, A_WIDTH:2 * A_WIDTH], kg_ref[gi:gi + 1, :]).astype(BF16)
        v = acc[:, 2 * A_WIDTH:].astype(BF16)
        n = tm // d
        for r in range(d):
            q_ref[0, r] = q[r * n:(r + 1) * n]
            k_ref[0, r] = k[r * n:(r + 1) * n]
            v_ref[0, r] = v[r * n:(r + 1) * n]


def _qkv_all(x, gain, w_qkv, q_gain, k_gain, tables, bd):
    B, S, D = x.shape
    tm = min(S, 512)
    G = len(SWA_GROUPS)
    tspec = pl.BlockSpec((1, tm, V7X_LANES), lambda b, n: (b, n, 0))
    const = lambda shape: pl.BlockSpec(shape, lambda b, n: (0,) * len(shape))
    out_specs, out_shape = [], []
    for _, d in SWA_GROUPS:
        out_specs += [pl.BlockSpec((1, d, tm // d, A_WIDTH), lambda b, n: (b, 0, n, 0))] * 3
        out_shape += [jax.ShapeDtypeStruct((B, d, S // d, A_WIDTH), BF16)] * 3
    ncol = D // V7X_LANES
    xspecs = [pl.BlockSpec((1, tm, V7X_LANES), functools.partial(lambda b, n, c: (b, n, c), c=c))
              for c in range(ncol)]
    outs = pl.pallas_call(
        functools.partial(_qkv_kernel, tm=tm, ncol=ncol),
        grid=(B, S // tm),
        in_specs=xspecs + [const((1, D)), const((D, G * 3 * A_WIDTH)), const((G, A_WIDTH)), const((G, A_WIDTH)),
                           tspec, tspec, tspec, const(bd.shape)],
        out_specs=out_specs,
        out_shape=out_shape,
        compiler_params=_params(("parallel", "parallel")),
        name="attn_qkv",
    )(*([x] * ncol), gain, w_qkv, q_gain, k_gain, *tables, bd)
    return [outs[3 * gi:3 * gi + 3] for gi in range(G)]


def _attn_kernel(q_ref, kc_ref, kp_ref, vc_ref, vp_ref, o_ref, lse_ref, kbuf, vbuf, *, tq):
    n = pl.program_id(2)
    kbuf[0:BAND] = kp_ref[0, 0]
    kbuf[BAND:] = kc_ref[0, 0]
    vbuf[0:BAND] = vp_ref[0, 0]
    vbuf[BAND:] = vc_ref[0, 0]

    qi = lax.broadcasted_iota(jnp.int32, (BAND, 2 * BAND), 0)
    kj = lax.broadcasted_iota(jnp.int32, (BAND, 2 * BAND), 1)
    band = jnp.where(kj >= qi, jnp.where(kj <= qi + BAND, 0.0, NEG_BIG), NEG_BIG)
    first_lo = jnp.where(n > 0, 0, BAND)
    band_first = jnp.where(kj >= first_lo, band, NEG_BIG)
    lane = lax.broadcasted_iota(jnp.int32, (BAND, V7X_LANES), 1)
    even = lane < A_HEAD_DIM
    lane_row = lax.broadcasted_iota(jnp.int32, (1, V7X_LANES), 1)
    head_masks = (jnp.where(lane_row < A_HEAD_DIM, 1.0, 0.0).astype(BF16),
                  jnp.where(lane_row < A_HEAD_DIM, 0.0, 1.0).astype(BF16))

    for j in range(tq // BAND):
        bias = band_first if j == 0 else band
        rows = slice(j * BAND, (j + 1) * BAND)
        lse_tile = jnp.zeros((BAND, V7X_LANES), F32)
        for p in range(A_WIDTH // V7X_LANES):
            cols = slice(p * V7X_LANES, (p + 1) * V7X_LANES)
            qp = q_ref[0, 0, rows, cols]
            kk = kbuf[j * BAND:(j + 2) * BAND, cols]
            vv = vbuf[j * BAND:(j + 2) * BAND, cols]
            outs = []
            for hi, hm in enumerate(head_masks):
                s = _dot_nt(qp * hm, kk) + bias
                m = jnp.max(s, axis=-1, keepdims=True)
                e = jnp.exp(s - m)
                l = jnp.sum(e, axis=-1, keepdims=True)
                o = _dot(e.astype(BF16), vv)
                outs.append(o * (1.0 / l))
                lse_tile = jnp.where(lane == 2 * p + hi, m + jnp.log(l), lse_tile)
            o_ref[0, 0, rows, cols] = jnp.where(even, outs[0], outs[1]).astype(BF16)
        lse_ref[0, 0, rows, :] = lse_tile


def _attn_group(q, k, v, dilation):
    B, d, L, W = q.shape
    tq = min(L, 512)
    nsub = tq // BAND
    cur = pl.BlockSpec((1, 1, tq, W), lambda b, r, n: (b, r, n, 0))
    prev = pl.BlockSpec((1, 1, BAND, W), lambda b, r, n: (b, r, jnp.maximum(n * nsub - 1, 0), 0))
    return pl.pallas_call(
        functools.partial(_attn_kernel, tq=tq),
        grid=(B, d, L // tq),
        in_specs=[cur, cur, prev, cur, prev],
        out_specs=[cur, pl.BlockSpec((1, 1, tq, V7X_LANES), lambda b, r, n: (b, r, n, 0))],
        out_shape=[jax.ShapeDtypeStruct((B, d, L, W), BF16),
                   jax.ShapeDtypeStruct((B, d, L, V7X_LANES), F32)],
        scratch_shapes=[pltpu.VMEM((BAND + tq, W), BF16), pltpu.VMEM((BAND + tq, W), BF16)],
        compiler_params=_params(("parallel", "parallel", "arbitrary")),
        name=f"band_attn_d{dilation}",
    )(q, k, k, v, v)


def _merge_out_kernel(*refs, tm):
    G = len(SWA_GROUPS)
    o_refs, l_refs = refs[:G], refs[G:2 * G]
    x_ref, w_ref, ex_ref, out_ref, o_scr, l_scr = refs[2 * G:]
    os, ls = [], []
    for gi, (_, d) in enumerate(SWA_GROUPS):
        if d == 1:
            os.append(o_refs[gi][0, 0].astype(F32))
            ls.append(l_refs[gi][0, 0])
            continue
        ncol = A_WIDTH // V7X_LANES
        for r in range(d):
            ov = o_refs[gi][0, r].astype(F32)
            for c in range(ncol):
                cols = slice(c * V7X_LANES, (c + 1) * V7X_LANES)
                o_scr[gi * ncol + c, pl.ds(r, tm // d, stride=d), :] = ov[:, cols]
            l_scr[gi, pl.ds(r, tm // d, stride=d), :] = l_refs[gi][0, r]
        os.append(jnp.concatenate([o_scr[gi * ncol + c] for c in range(ncol)], axis=1))
        ls.append(l_scr[gi])
    m = functools.reduce(jnp.maximum, ls)
    es = [jnp.exp(a - m) for a in ls]
    inv = 1.0 / functools.reduce(lambda a, b: a + b, es)
    merged = None
    for e, o in zip(es, os):
        wgt = e * inv
        hi = wgt.astype(BF16)
        lo = (wgt - hi.astype(F32)).astype(BF16)
        term = (_dot(hi, ex_ref[...]) + _dot(lo, ex_ref[...])) * o
        merged = term if merged is None else merged + term
    out_ref[0] = x_ref[0] + _dot(merged.astype(BF16), w_ref[...])


def _merge_out(os, ls, x, w_o):
    B, S, D = x.shape
    tm = min(S, 512)
    G = len(SWA_GROUPS)
    gspecs = [pl.BlockSpec((1, d, tm // d, A_WIDTH), lambda b, n: (b, 0, n, 0)) for _, d in SWA_GROUPS]
    lspecs = [pl.BlockSpec((1, d, tm // d, V7X_LANES), lambda b, n: (b, 0, n, 0)) for _, d in SWA_GROUPS]
    xspec = pl.BlockSpec((1, tm, D), lambda b, n: (b, n, 0))
    expand = jnp.pad(jnp.kron(jnp.eye(A_HEADS, dtype=F32), jnp.ones((1, A_HEAD_DIM), F32)),
                     ((0, V7X_LANES - A_HEADS), (0, 0))).astype(BF16)
    return pl.pallas_call(
        functools.partial(_merge_out_kernel, tm=tm),
        grid=(B, S // tm),
        in_specs=gspecs + lspecs + [xspec, pl.BlockSpec((A_WIDTH, D), lambda b, n: (0, 0)),
                                    pl.BlockSpec((V7X_LANES, A_WIDTH), lambda b, n: (0, 0))],
        out_specs=xspec,
        out_shape=jax.ShapeDtypeStruct((B, S, D), F32),
        scratch_shapes=[pltpu.VMEM((G * A_WIDTH // V7X_LANES, tm, V7X_LANES), F32),
                        pltpu.VMEM((G, tm, V7X_LANES), F32)],
        compiler_params=_params(("parallel", "parallel")),
        name="attn_merge_out",
    )(*os, *ls, x, w_o, expand)


def _mlp_kernel(*refs, has_mix):
    if has_mix:
        a_ref, wmix_ref = refs[:2]
        refs = refs[2:]
    x_ref, g_ref, wup_ref, wdn_ref, p_ref, g2_ref, wple_ref, wgate_ref, out_ref, hn_scr, acc_scr = refs
    j = pl.program_id(1)

    @pl.when(j == 0)
    def _():
        x1 = x_ref[...]
        if has_mix:
            x1 = x1 + _dot(a_ref[...], wmix_ref[...])
        hn_scr[...] = _rms(x1, g_ref[...]).astype(BF16)
        acc_scr[...] = x1

    h = jnp.maximum(_dot(hn_scr[...], wup_ref[...]), 0.0)
    acc_scr[...] += _dot((h * h).astype(BF16), wdn_ref[...])

    @pl.when(j == pl.num_programs(1) - 1)
    def _():
        x2 = acc_scr[...]
        hn2 = _rms(x2, g2_ref[...]).astype(BF16)
        gate = jax.nn.sigmoid(_dot(hn2, wgate_ref[...]))
        ple = _dot(p_ref[...].astype(BF16), wple_ref[...])
        out_ref[...] = x2 + ple * gate


def _mlp(x2d, gain, w_up, w_down, p2d, gain2, w_ple, w_gate, mix=None, w_mix=None):
    T, D = x2d.shape
    FF = w_up.shape[1]
    P = p2d.shape[1]
    tm = min(T, 1024)
    tf = min(FF, 1024)
    xspec = pl.BlockSpec((tm, D), lambda i, j: (i, 0))
    const = lambda shape: pl.BlockSpec(shape, lambda i, j: (0, 0))
    has_mix = mix is not None
    mix_specs = [pl.BlockSpec((tm, mix.shape[1]), lambda i, j: (i, 0)), const(w_mix.shape)] if has_mix else []
    mix_args = [mix, w_mix] if has_mix else []
    return pl.pallas_call(
        functools.partial(_mlp_kernel, has_mix=has_mix),
        grid=(T // tm, FF // tf),
        in_specs=mix_specs + [xspec, const((1, D)),
                  pl.BlockSpec((D, tf), lambda i, j: (0, j)),
                  pl.BlockSpec((tf, D), lambda i, j: (j, 0)),
                  pl.BlockSpec((tm, P), lambda i, j: (i, 0)),
                  const((1, D)), const((P, D)), const((D, D))],
        out_specs=xspec,
        out_shape=jax.ShapeDtypeStruct((T, D), F32),
        scratch_shapes=[pltpu.VMEM((tm, D), BF16), pltpu.VMEM((tm, D), F32)],
        compiler_params=_params(("parallel", "arbitrary")),
        name="channel_mixer",
    )(*mix_args, x2d, gain, w_up, w_down, p2d, gain2, w_ple, w_gate)


def _seg_cumsum(x, axis, reverse=False):
    n = x.shape[axis]
    pos = lax.broadcasted_iota(jnp.int32, x.shape, axis) & (CHUNK - 1)
    step = 1
    if reverse:
        acc = jnp.where(pos < CHUNK - 1, pltpu.roll(x, n - 1, axis), 0.0)
        x = acc
        while step < CHUNK:
            x = x + jnp.where(pos < CHUNK - step, pltpu.roll(x, n - step, axis), 0.0)
            step *= 2
        return x
    while step < CHUNK:
        x = x + jnp.where(pos >= step, pltpu.roll(x, step, axis), 0.0)
        step *= 2
    return x


def _softplus(x):
    return jnp.maximum(x, 0.0) + jnp.log1p(jnp.exp(-jnp.abs(x)))


def _dn_proj_kernel(x_ref, g_ref, wqkv_ref, wz_ref, wabt_ref, conv_ref, alog_c, dtb_c,
                    k_ref, kb_ref, vb_ref, kbg_ref, q_ref, qd_ref, kd_ref, z_ref, gcol_ref, grow_ref,
                    y_scr, *, tm):
    s = pl.program_id(1)
    H, E = DN_HEADS, DN_HEAD_DIM
    PAD = V7X_SUBLANES

    @pl.when(s == 0)
    def _():
        y_scr[0:PAD] = jnp.zeros((PAD, 3 * DN_WIDTH), F32)

    hn = _rms(x_ref[0], g_ref[...]).astype(BF16)
    y_scr[PAD:] = _dot(hn, wqkv_ref[...])
    z_ref[0] = _dot(hn, wz_ref[...]).astype(BF16)
    abt = _dot_nt(wabt_ref[...], hn)

    g_row = -jnp.exp(alog_c[...]) * _softplus(abt[:H] + dtb_c[...])
    beta_row = jax.nn.sigmoid(abt[H:])
    gam_row = _seg_cumsum(g_row, 1)
    suf_row = _seg_cumsum(g_row, 1, reverse=True)
    grow_ref[0] = gam_row
    rows = jnp.concatenate([gam_row, beta_row, jnp.exp(gam_row), jnp.exp(suf_row),
                            jnp.zeros((V7X_LANES - 4 * H, tm), F32)], axis=0)
    cols_t = rows.T
    gcol_ref[0] = cols_t

    def conv_silu(cb):
        cols = slice(cb * E, (cb + 1) * E)
        ext = y_scr[:, cols]
        y = conv_ref[CONV_WIDTH - 1:CONV_WIDTH, cols] * ext[PAD:]
        for back in range(1, CONV_WIDTH):
            tap = conv_ref[CONV_WIDTH - 1 - back:CONV_WIDTH - back, cols]
            y = y + tap * pltpu.roll(ext, back, 0)[PAD:]
        return y * jax.nn.sigmoid(y)

    def l2n(y, scale=1.0):
        return y * (lax.rsqrt(jnp.sum(y * y, axis=-1, keepdims=True) + EPS) * scale)

    for h in range(H):
        cols = slice(h * E, (h + 1) * E)
        q = l2n(conv_silu(h), E ** -0.5)
        k = l2n(conv_silu(H + h))
        v = conv_silu(2 * H + h)
        beta = cols_t[:, H + h:H + h + 1]
        eg = cols_t[:, 2 * H + h:2 * H + h + 1]
        kb = k * beta
        k_ref[0, :, cols] = k.astype(BF16)
        kb_ref[0, :, cols] = kb.astype(BF16)
        vb_ref[0, :, cols] = (v * beta).astype(BF16)
        kbg_ref[0, :, cols] = (kb * eg).astype(BF16)
        q_ref[0, :, cols] = q.astype(BF16)
        qd_ref[0, :, cols] = (q * eg).astype(BF16)
        kd_ref[0, :, cols] = (k * cols_t[:, 3 * H + h:3 * H + h + 1]).astype(BF16)

    y_scr[0:PAD] = y_scr[tm:tm + PAD]


def _dn_proj(x, gain, w_qkv, w_z, w_abt, conv_w, alog_c, dtb_c):
    B, S, D = x.shape
    tm = min(S, 256)
    W = DN_WIDTH
    big = jax.ShapeDtypeStruct((B, S, W), BF16)
    bspec = pl.BlockSpec((1, tm, W), lambda b, s: (b, s, 0))
    const = lambda shape: pl.BlockSpec(shape, lambda b, s: (0,) * len(shape))
    return pl.pallas_call(
        functools.partial(_dn_proj_kernel, tm=tm),
        grid=(B, S // tm),
        in_specs=[pl.BlockSpec((1, tm, D), lambda b, s: (b, s, 0)), const((1, D)),
                  const((D, 3 * W)), const((D, W)), const((2 * DN_HEADS, D)),
                  const((CONV_WIDTH, 3 * W)), const((DN_HEADS, 1)), const((DN_HEADS, 1))],
        out_specs=[bspec] * 8 + [pl.BlockSpec((1, tm, V7X_LANES), lambda b, s: (b, s, 0)),
                                 pl.BlockSpec((1, DN_HEADS, tm), lambda b, s: (b, 0, s))],
        out_shape=[big] * 8 + [jax.ShapeDtypeStruct((B, S, V7X_LANES), F32),
                               jax.ShapeDtypeStruct((B, DN_HEADS, S), F32)],
        scratch_shapes=[pltpu.VMEM((tm + V7X_SUBLANES, 3 * W), F32)],
        compiler_params=_params(("parallel", "arbitrary")),
        name="deltanet_proj",
    )(x, gain, w_qkv, w_z, w_abt, conv_w, alog_c, dtb_c)


def _delta_kernel(k_ref, kb_ref, vb_ref, kbg_ref, q_ref, qd_ref, kd_ref, z_ref, gcol_ref, grow_ref, og_ref,
                  out_ref, state, *, tc, nh):
    hb = pl.program_id(1)
    n = pl.program_id(2)
    C, E = CHUNK, DN_HEAD_DIM
    nc = tc // C

    @pl.when(n == 0)
    def _():
        state[...] = jnp.zeros_like(state)

    lane = lax.broadcasted_iota(jnp.int32, (tc, V7X_LANES), 1)
    ri = lax.broadcasted_iota(jnp.int32, (C, C), 0)
    ci = lax.broadcasted_iota(jnp.int32, (C, C), 1)
    lower = ri >= ci
    strict = ri > ci
    gam_c, gam_r = [], []
    for i in range(nh):
        h = hb * nh + i
        gam_c.append(jnp.sum(jnp.where(lane == h, gcol_ref[0], 0.0), axis=-1, keepdims=True))
        gam_r.append(grow_ref[0, pl.ds(h, 1), :])

    units = [(i, c) for i in range(nh) for c in range(nc)]
    blk = lambda ref, i, c: ref[0, c * C:(c + 1) * C, i * E:(i + 1) * E]

    attn, pw = [], []
    for i, c in units:
        rows = slice(c * C, (c + 1) * C)
        diff = gam_c[i][rows] - gam_r[i][:, rows]
        decay = jnp.exp(jnp.where(lower, diff, NEG_BIG))
        gram = _dot_nt(jnp.concatenate([blk(q_ref, i, c), blk(kb_ref, i, c)], axis=0), blk(k_ref, i, c))
        attn.append((gram[:C] * decay).astype(BF16))
        pw.append(-jnp.where(strict, gram[C:] * decay, 0.0))
    nm = list(pw)
    for _ in range(int(math.log2(C)) - 1):
        pwb = [x.astype(BF16) for x in pw]
        pw = [_dot(x, x) for x in pwb]
        pwb = [x.astype(BF16) for x in pw]
        nm = [a + b + _dot(a.astype(BF16), bb) for a, b, bb in zip(nm, pw, pwb)]
    u, w = [], []
    for (i, c), a in zip(units, nm):
        rhs = jnp.concatenate([blk(vb_ref, i, c), blk(kbg_ref, i, c)], axis=1)
        sol = rhs.astype(F32) + _dot(a.astype(BF16), rhs)
        u.append(sol[:, :E])
        w.append(sol[:, E:].astype(BF16))

    S = [state[i] for i in range(nh)]
    for c in range(nc):
        rows = slice(c * C, (c + 1) * C)
        ws = [_dot(jnp.concatenate([w[i * nc + c], blk(qd_ref, i, c)], axis=0), S[i].astype(BF16))
              for i in range(nh)]
        vnb = [(u[i * nc + c] - ws[i][:C]).astype(BF16) for i in range(nh)]
        o = [ws[i][C:] + _dot(attn[i * nc + c], vnb[i]) for i in range(nh)]
        S = [S[i] * jnp.exp(gam_c[i][c * C + C - 1:c * C + C, :]) + _dot_tn(blk(kd_ref, i, c), vnb[i])
             for i in range(nh)]
        for i in range(nh):
            zc = blk(z_ref, i, c).astype(F32)
            out_ref[0, rows, i * E:(i + 1) * E] = (
                _rms(o[i], og_ref[...]) * (zc * jax.nn.sigmoid(zc))).astype(BF16)
    for i in range(nh):
        state[i] = S[i]


def _delta_rule(k, kb, vb, kbg, q, qd, kd, z, gcol, grow, o_gain):
    B, S, W = k.shape
    E = DN_HEAD_DIM
    tc = min(S, 256)
    nh = DN_HEADS
    hspec = pl.BlockSpec((1, tc, nh * E), lambda b, h, n: (b, n, h))
    return pl.pallas_call(
        functools.partial(_delta_kernel, tc=tc, nh=nh),
        grid=(B, DN_HEADS // nh, S // tc),
        in_specs=[hspec] * 8 + [pl.BlockSpec((1, tc, V7X_LANES), lambda b, h, n: (b, n, 0)),
                                pl.BlockSpec((1, DN_HEADS, tc), lambda b, h, n: (b, 0, n)),
                                pl.BlockSpec((1, E), lambda b, h, n: (0, 0))],
        out_specs=hspec,
        out_shape=jax.ShapeDtypeStruct((B, S, W), BF16),
        scratch_shapes=[pltpu.VMEM((nh, E, E), F32)],
        compiler_params=_params(("parallel", "parallel", "arbitrary")),
        name="delta_rule",
    )(k, kb, vb, kbg, q, qd, kd, z, gcol, grow, o_gain)


def _attention_layer(x, tables, mix_gain, w_qkv, q_gain, k_gain, w_o):
    B, S, D = x.shape
    gain = mix_gain[None, :]
    heads_per_pass = V7X_MXU_DIM // A_HEAD_DIM
    bd = jnp.kron(jnp.eye(heads_per_pass, dtype=F32),
                  jnp.full((A_HEAD_DIM, A_HEAD_DIM), 1.0 / A_HEAD_DIM, F32)).astype(BF16)
    q_gain = q_gain * (A_HEAD_DIM ** -0.5)
    assert all(window // dilation == BAND for window, dilation in SWA_GROUPS)
    qkv = _qkv_all(x, gain, w_qkv.astype(BF16), jnp.tile(q_gain, (1, A_HEADS)), jnp.tile(k_gain, (1, A_HEADS)),
                   tables, bd)
    os, ls = [], []
    for (q, k, v), (_, dilation) in zip(qkv, SWA_GROUPS):
        o, lse = _attn_group(q, k, v, dilation)
        os.append(o)
        ls.append(lse)
    return _merge_out(os, ls, x, w_o.astype(BF16))


def _deltanet_layer(x, mix_gain, w_in, conv_w, a_log, dt_bias, o_gain):
    B, S, D = x.shape
    W, H = DN_WIDTH, DN_HEADS
    c = 3 * W
    w_qkv = w_in[:, :c].astype(BF16)
    w_a = w_in[:, c:c + H]
    w_b = w_in[:, c + H:c + 2 * H]
    w_z = w_in[:, c + 2 * H:].astype(BF16)
    w_abt = jnp.concatenate([w_a, w_b], axis=1).T.astype(BF16)
    outs = _dn_proj(x, mix_gain[None, :], w_qkv, w_z, w_abt, conv_w, a_log[:, None], dt_bias[:, None])
    return _delta_rule(*outs, o_gain[None, :]).reshape(B * S, W)


def kernel(x, p, positions, mix_norm, attn_w_qkv, attn_q_gain, attn_k_gain, attn_w_o, dn_w_in, dn_conv,
           dn_a_log, dn_dt_bias, dn_o_gain, dn_w_o, mlp_norm, w_up, w_down, ple_norm, w_ple, w_ple_gate):
    B, S, D = x.shape
    depth = p.shape[0]
    tables = _rope_tables(positions)
    tables = [t.reshape(B, S, V7X_LANES) for t in tables]
    for i in range(depth):
        j = i // 2
        mix, w_mix = None, None
        if i % 2 == 0:
            x = _attention_layer(x, tables, mix_norm[i], attn_w_qkv[j], attn_q_gain[j], attn_k_gain[j],
                                 attn_w_o[j])
        else:
            mix = _deltanet_layer(x, mix_norm[i], dn_w_in[j], dn_conv[j], dn_a_log[j], dn_dt_bias[j],
                                  dn_o_gain[j])
            w_mix = dn_w_o[j].astype(BF16)
        x = _mlp(x.reshape(B * S, D), mlp_norm[i][None, :], w_up[i].astype(BF16), w_down[i].astype(BF16),
                 p[i].reshape(B * S, -1), ple_norm[i][None, :], w_ple[i].astype(BF16),
                 w_ple_gate[i].astype(BF16), mix=mix, w_mix=w_mix).reshape(B, S, D)
    return x
```

```python
import functools
import math

import jax
import jax.numpy as jnp
from jax import lax
from jax.experimental import pallas as pl
from jax.experimental.pallas import tpu as pltpu

F32 = jnp.float32
BF16 = jnp.bfloat16

EPS = 1e-6
SWA_GROUPS = ((128, 1), (512, 4), (2048, 16))
A_HEADS = 8
A_HEAD_DIM = 64
A_WIDTH = A_HEADS * A_HEAD_DIM
ROPE_DIM = A_HEAD_DIM // 4
ROPE_HALF = ROPE_DIM // 2
ROPE_THETA = 500000.0
BAND = 128
DN_HEADS = 8
DN_HEAD_DIM = 128
DN_WIDTH = DN_HEADS * DN_HEAD_DIM
CONV_WIDTH = 4
CHUNK = 64
NEG_BIG = -1e30

V7X_LANES = 128
V7X_SUBLANES = 8
V7X_MXU_DIM = 256
V7X_VMEM_LIMIT = 56 * 1024 * 1024


def _params(semantics):
    return pltpu.CompilerParams(dimension_semantics=semantics, vmem_limit_bytes=V7X_VMEM_LIMIT)


def _rms(x, gain):
    ms = jnp.mean(x * x, axis=-1, keepdims=True)
    return x * lax.rsqrt(ms + EPS) * gain


def _silu(y):
    h = 0.5 * y
    return h + h * jnp.tanh(h)


def _dot(a, b):
    return jnp.dot(a, b, preferred_element_type=F32)


def _dot_nt(a, b):
    return lax.dot_general(a, b, (((1,), (1,)), ((), ())), preferred_element_type=F32)


def _dot_tn(a, b):
    return lax.dot_general(a, b, (((0,), (0,)), ((), ())), preferred_element_type=F32)


def _rope_tables_kernel(pos_ref, invf_ref, c_ref, s1_ref, s2_ref):
    ang = pos_ref[...].astype(F32) * invf_ref[...]
    e = lax.broadcasted_iota(jnp.int32, ang.shape, 1) & (A_HEAD_DIM - 1)
    cos = jnp.cos(ang)
    sin = jnp.sin(ang)
    c_ref[...] = jnp.where(e < ROPE_DIM, cos, 1.0)
    s1_ref[...] = jnp.where(e < ROPE_HALF, -sin, 0.0)
    s2_ref[...] = jnp.where(e < ROPE_HALF, 0.0, jnp.where(e < ROPE_DIM, sin, 0.0))


def _rope_tables(positions):
    T = positions.size
    tm = min(T, 2048)
    inv_freq = ROPE_THETA ** (-jnp.arange(0, ROPE_DIM, 2, dtype=F32) / ROPE_DIM)
    invf = jnp.tile(inv_freq, V7X_LANES // ROPE_HALF)[None, :]
    out = jax.ShapeDtypeStruct((T, V7X_LANES), F32)
    spec = pl.BlockSpec((tm, V7X_LANES), lambda i: (i, 0))
    return pl.pallas_call(
        _rope_tables_kernel,
        grid=(T // tm,),
        in_specs=[pl.BlockSpec((tm, 1), lambda i: (i, 0)),
                  pl.BlockSpec((1, V7X_LANES), lambda i: (0, 0))],
        out_specs=[spec, spec, spec],
        out_shape=[out, out, out],
        compiler_params=_params(("parallel",)),
        name="rope_tables",
    )(positions.reshape(T, 1), invf)


def _by_residue(ref, tm, d):
    if d == 1:
        return ref[0]
    return jnp.concatenate([ref[0, pl.ds(r, tm // d, stride=d), :] for r in range(d)], axis=0)


def _qkv_kernel(*refs, tm, ncol):
    x_refs = refs[:ncol]
    g_ref, w_ref, qg_ref, kg_ref, c_ref, s1_ref, s2_ref, bd_ref = refs[ncol:ncol + 8]
    out_refs = refs[ncol + 8:]
    reps = A_WIDTH // V7X_LANES
    bdw = bd_ref.shape[0]
    for gi, (_, d) in enumerate(SWA_GROUPS):
        q_ref, k_ref, v_ref = out_refs[3 * gi:3 * gi + 3]
        xg = jnp.concatenate([_by_residue(xr, tm, d) for xr in x_refs], axis=1)
        hn = _rms(xg, g_ref[...]).astype(BF16)
        acc = _dot(hn, w_ref[:, 3 * A_WIDTH * gi:3 * A_WIDTH * (gi + 1)])
        c = jnp.concatenate([_by_residue(c_ref, tm, d)] * reps, axis=1)
        s1 = jnp.concatenate([_by_residue(s1_ref, tm, d)] * reps, axis=1)
        s2 = jnp.concatenate([_by_residue(s2_ref, tm, d)] * reps, axis=1)

        def norm_rope(y, gain):
            y2 = (y * y).astype(BF16)
            ms = jnp.concatenate([_dot(y2[:, i:i + bdw], bd_ref[...]) for i in range(0, A_WIDTH, bdw)], axis=1)
            yn = y * lax.rsqrt(ms + EPS) * gain
            hi = pltpu.roll(yn, A_WIDTH - ROPE_HALF, 1)
            lo = pltpu.roll(yn, ROPE_HALF, 1)
            return yn * c + hi * s1 + lo * s2

        q = norm_rope(acc[:, :A_WIDTH], qg_ref[gi:gi + 1, :]).astype(BF16)
        k = norm_rope(acc[:, A_WIDTH:2 * A_WIDTH], kg_ref[gi:gi + 1, :]).astype(BF16)
        v = acc[:, 2 * A_WIDTH:].astype(BF16)
        n = tm // d
        for r in range(d):
            q_ref[0, r] = q[r * n:(r + 1) * n]
            k_ref[0, r] = k[r * n:(r + 1) * n]
            v_ref[0, r] = v[r * n:(r + 1) * n]


def _qkv_all(x, gain, w_qkv, q_gain, k_gain, tables, bd):
    B, S, D = x.shape
    tm = min(S, 512)
    G = len(SWA_GROUPS)
    tspec = pl.BlockSpec((1, tm, V7X_LANES), lambda b, n: (b, n, 0))
    const = lambda shape: pl.BlockSpec(shape, lambda b, n: (0,) * len(shape))
    out_specs, out_shape = [], []
    for _, d in SWA_GROUPS:
        out_specs += [pl.BlockSpec((1, d, tm // d, A_WIDTH), lambda b, n: (b, 0, n, 0))] * 3
        out_shape += [jax.ShapeDtypeStruct((B, d, S // d, A_WIDTH), BF16)] * 3
    ncol = D // V7X_LANES
    xspecs = [pl.BlockSpec((1, tm, V7X_LANES), functools.partial(lambda b, n, c: (b, n, c), c=c))
              for c in range(ncol)]
    outs = pl.pallas_call(
        functools.partial(_qkv_kernel, tm=tm, ncol=ncol),
        grid=(B, S // tm),
        in_specs=xspecs + [const((1, D)), const((D, G * 3 * A_WIDTH)), const((G, A_WIDTH)), const((G, A_WIDTH)),
                           tspec, tspec, tspec, const(bd.shape)],
        out_specs=out_specs,
        out_shape=out_shape,
        compiler_params=_params(("parallel", "parallel")),
        name="attn_qkv",
    )(*([x] * ncol), gain, w_qkv, q_gain, k_gain, *tables, bd)
    return [outs[3 * gi:3 * gi + 3] for gi in range(G)]


def _attn_kernel(q_ref, kc_ref, kp_ref, vc_ref, vp_ref, o_ref, lse_ref, *, tq):
    n = pl.program_id(2)

    def window(cur_ref, prev_ref, j, cols):
        if j == 0:
            return jnp.concatenate([prev_ref[0, 0, :, cols], cur_ref[0, 0, 0:BAND, cols]], axis=0)
        return cur_ref[0, 0, (j - 1) * BAND:(j + 1) * BAND, cols]

    qi = lax.broadcasted_iota(jnp.int32, (BAND, 2 * BAND), 0)
    kj = lax.broadcasted_iota(jnp.int32, (BAND, 2 * BAND), 1)
    band = jnp.where(kj >= qi, jnp.where(kj <= qi + BAND, 0.0, NEG_BIG), NEG_BIG)
    first_lo = jnp.where(n > 0, 0, BAND)
    band_first = jnp.where(kj >= first_lo, band, NEG_BIG)
    lane = lax.broadcasted_iota(jnp.int32, (BAND, V7X_LANES), 1)
    even = lane < A_HEAD_DIM
    lane_row = lax.broadcasted_iota(jnp.int32, (1, V7X_LANES), 1)
    head_masks = (jnp.where(lane_row < A_HEAD_DIM, 1.0, 0.0).astype(BF16),
                  jnp.where(lane_row < A_HEAD_DIM, 0.0, 1.0).astype(BF16))

    for j in range(tq // BAND):
        bias = band_first if j == 0 else band
        rows = slice(j * BAND, (j + 1) * BAND)
        lse_tile = jnp.zeros((BAND, V7X_LANES), F32)
        for p in range(A_WIDTH // V7X_LANES):
            cols = slice(p * V7X_LANES, (p + 1) * V7X_LANES)
            qp = q_ref[0, 0, rows, cols]
            kk = window(kc_ref, kp_ref, j, cols)
            vv = window(vc_ref, vp_ref, j, cols)
            outs = []
            for hi, hm in enumerate(head_masks):
                s = _dot_nt(qp * hm, kk) + bias
                m = jnp.max(s, axis=-1, keepdims=True)
                e = jnp.exp(s - m)
                l = jnp.sum(e, axis=-1, keepdims=True)
                o = _dot(e.astype(BF16), vv)
                outs.append(o * (1.0 / l))
                lse_tile = jnp.where(lane == 2 * p + hi, m + jnp.log(l), lse_tile)
            o_ref[0, 0, rows, cols] = jnp.where(even, outs[0], outs[1]).astype(BF16)
        lse_ref[0, 0, rows, :] = lse_tile


def _attn_group(q, k, v, dilation):
    B, d, L, W = q.shape
    tq = min(L, 512)
    nsub = tq // BAND
    cur = pl.BlockSpec((1, 1, tq, W), lambda b, r, n: (b, r, n, 0))
    prev = pl.BlockSpec((1, 1, BAND, W), lambda b, r, n: (b, r, jnp.maximum(n * nsub - 1, 0), 0))
    return pl.pallas_call(
        functools.partial(_attn_kernel, tq=tq),
        grid=(B, d, L // tq),
        in_specs=[cur, cur, prev, cur, prev],
        out_specs=[cur, pl.BlockSpec((1, 1, tq, V7X_LANES), lambda b, r, n: (b, r, n, 0))],
        out_shape=[jax.ShapeDtypeStruct((B, d, L, W), BF16),
                   jax.ShapeDtypeStruct((B, d, L, V7X_LANES), F32)],
        compiler_params=_params(("parallel", "parallel", "arbitrary")),
        name=f"band_attn_d{dilation}",
    )(q, k, k, v, v)


def _merge_out_kernel(*refs, tm):
    G = len(SWA_GROUPS)
    o_refs, l_refs = refs[:G], refs[G:2 * G]
    x_ref, w_ref, ex_ref, out_ref, o_scr, l_scr = refs[2 * G:]
    os, ls = [], []
    for gi, (_, d) in enumerate(SWA_GROUPS):
        if d == 1:
            os.append(o_refs[gi][0, 0].astype(F32))
            ls.append(l_refs[gi][0, 0])
            continue
        ncol = A_WIDTH // V7X_LANES
        for r in range(d):
            ov = o_refs[gi][0, r].astype(F32)
            for c in range(ncol):
                cols = slice(c * V7X_LANES, (c + 1) * V7X_LANES)
                o_scr[gi * ncol + c, pl.ds(r, tm // d, stride=d), :] = ov[:, cols]
            l_scr[gi, pl.ds(r, tm // d, stride=d), :] = l_refs[gi][0, r]
        os.append(jnp.concatenate([o_scr[gi * ncol + c] for c in range(ncol)], axis=1))
        ls.append(l_scr[gi])
    m = functools.reduce(jnp.maximum, ls)
    es = [jnp.exp(a - m) for a in ls]
    inv = 1.0 / functools.reduce(lambda a, b: a + b, es)
    merged = None
    for e, o in zip(es, os):
        wgt = e * inv
        hi = wgt.astype(BF16)
        lo = (wgt - hi.astype(F32)).astype(BF16)
        term = (_dot(hi, ex_ref[...]) + _dot(lo, ex_ref[...])) * o
        merged = term if merged is None else merged + term
    out_ref[0] = x_ref[0] + _dot(merged.astype(BF16), w_ref[...])


def _merge_out(os, ls, x, w_o):
    B, S, D = x.shape
    tm = min(S, 512)
    G = len(SWA_GROUPS)
    gspecs = [pl.BlockSpec((1, d, tm // d, A_WIDTH), lambda b, n: (b, 0, n, 0)) for _, d in SWA_GROUPS]
    lspecs = [pl.BlockSpec((1, d, tm // d, V7X_LANES), lambda b, n: (b, 0, n, 0)) for _, d in SWA_GROUPS]
    xspec = pl.BlockSpec((1, tm, D), lambda b, n: (b, n, 0))
    expand = jnp.pad(jnp.kron(jnp.eye(A_HEADS, dtype=F32), jnp.ones((1, A_HEAD_DIM), F32)),
                     ((0, V7X_LANES - A_HEADS), (0, 0))).astype(BF16)
    return pl.pallas_call(
        functools.partial(_merge_out_kernel, tm=tm),
        grid=(B, S // tm),
        in_specs=gspecs + lspecs + [xspec, pl.BlockSpec((A_WIDTH, D), lambda b, n: (0, 0)),
                                    pl.BlockSpec((V7X_LANES, A_WIDTH), lambda b, n: (0, 0))],
        out_specs=xspec,
        out_shape=jax.ShapeDtypeStruct((B, S, D), F32),
        scratch_shapes=[pltpu.VMEM((G * A_WIDTH // V7X_LANES, tm, V7X_LANES), F32),
                        pltpu.VMEM((G, tm, V7X_LANES), F32)],
        compiler_params=_params(("parallel", "parallel")),
        name="attn_merge_out",
    )(*os, *ls, x, w_o, expand)


def _mlp_kernel(*refs, has_mix):
    if has_mix:
        a_ref, wmix_ref = refs[:2]
        refs = refs[2:]
    x_ref, g_ref, wup_ref, wdn_ref, p_ref, g2_ref, wple_ref, wgate_ref, out_ref, hn_scr, acc_scr = refs
    j = pl.program_id(1)

    @pl.when(j == 0)
    def _():
        x1 = x_ref[...]
        if has_mix:
            x1 = x1 + _dot(a_ref[...], wmix_ref[...])
        hn_scr[...] = _rms(x1, g_ref[...]).astype(BF16)
        acc_scr[...] = x1

    h = jnp.maximum(_dot(hn_scr[...], wup_ref[...]), 0.0)
    acc_scr[...] += _dot((h * h).astype(BF16), wdn_ref[...])

    @pl.when(j == pl.num_programs(1) - 1)
    def _():
        x2 = acc_scr[...]
        hn2 = _rms(x2, g2_ref[...]).astype(BF16)
        gate = jax.nn.sigmoid(_dot(hn2, wgate_ref[...]))
        ple = _dot(p_ref[...].astype(BF16), wple_ref[...])
        out_ref[...] = x2 + ple * gate


def _mlp(x2d, gain, w_up, w_down, p2d, gain2, w_ple, w_gate, mix=None, w_mix=None):
    T, D = x2d.shape
    FF = w_up.shape[1]
    P = p2d.shape[1]
    tm = min(T, 1024)
    tf = min(FF, 1024)
    xspec = pl.BlockSpec((tm, D), lambda i, j: (i, 0))
    const = lambda shape: pl.BlockSpec(shape, lambda i, j: (0, 0))
    has_mix = mix is not None
    mix_specs = [pl.BlockSpec((tm, mix.shape[1]), lambda i, j: (i, 0)), const(w_mix.shape)] if has_mix else []
    mix_args = [mix, w_mix] if has_mix else []
    return pl.pallas_call(
        functools.partial(_mlp_kernel, has_mix=has_mix),
        grid=(T // tm, FF // tf),
        in_specs=mix_specs + [xspec, const((1, D)),
                  pl.BlockSpec((D, tf), lambda i, j: (0, j)),
                  pl.BlockSpec((tf, D), lambda i, j: (j, 0)),
                  pl.BlockSpec((tm, P), lambda i, j: (i, 0)),
                  const((1, D)), const((P, D)), const((D, D))],
        out_specs=xspec,
        out_shape=jax.ShapeDtypeStruct((T, D), F32),
        scratch_shapes=[pltpu.VMEM((tm, D), BF16), pltpu.VMEM((tm, D), F32)],
        compiler_params=_params(("parallel", "arbitrary")),
        name="channel_mixer",
    )(*mix_args, x2d, gain, w_up, w_down, p2d, gain2, w_ple, w_gate)


def _seg_cumsum(x, axis, reverse=False):
    n = x.shape[axis]
    pos = lax.broadcasted_iota(jnp.int32, x.shape, axis) & (CHUNK - 1)
    step = 1
    if reverse:
        acc = jnp.where(pos < CHUNK - 1, pltpu.roll(x, n - 1, axis), 0.0)
        x = acc
        while step < CHUNK:
            x = x + jnp.where(pos < CHUNK - step, pltpu.roll(x, n - step, axis), 0.0)
            step *= 2
        return x
    while step < CHUNK:
        x = x + jnp.where(pos >= step, pltpu.roll(x, step, axis), 0.0)
        step *= 2
    return x


def _softplus(x):
    return jnp.maximum(x, 0.0) + jnp.log1p(jnp.exp(-jnp.abs(x)))


def _dn_proj_kernel(x_ref, g_ref, wqkv_ref, wz_ref, wabt_ref, conv_ref, alog_c, dtb_c,
                    k_ref, kb_ref, vb_ref, kbg_ref, q_ref, qd_ref, kd_ref, z_ref, gcol_ref, grow_ref,
                    y_scr, *, tm):
    s = pl.program_id(1)
    H, E = DN_HEADS, DN_HEAD_DIM
    PAD = V7X_SUBLANES

    @pl.when(s == 0)
    def _():
        y_scr[0:PAD] = jnp.zeros((PAD, 3 * DN_WIDTH), F32)

    hn = _rms(x_ref[0], g_ref[...]).astype(BF16)
    y_scr[PAD:] = _dot(hn, wqkv_ref[...])
    z_ref[0] = _dot(hn, wz_ref[...]).astype(BF16)
    abt = _dot_nt(wabt_ref[...], hn)

    g_row = -jnp.exp(alog_c[...]) * _softplus(abt[:H] + dtb_c[...])
    beta_row = jax.nn.sigmoid(abt[H:])
    gam_row = _seg_cumsum(g_row, 1)
    suf_row = _seg_cumsum(g_row, 1, reverse=True)
    grow_ref[0] = gam_row
    rows = jnp.concatenate([gam_row, beta_row, jnp.exp(gam_row), jnp.exp(suf_row),
                            jnp.zeros((V7X_LANES - 4 * H, tm), F32)], axis=0)
    cols_t = rows.T
    gcol_ref[0] = cols_t

    def conv_silu(cb):
        cols = slice(cb * E, (cb + 1) * E)
        ext = y_scr[:, cols]
        y = conv_ref[CONV_WIDTH - 1:CONV_WIDTH, cols] * ext[PAD:]
        for back in range(1, CONV_WIDTH):
            tap = conv_ref[CONV_WIDTH - 1 - back:CONV_WIDTH - back, cols]
            y = y + tap * pltpu.roll(ext, back, 0)[PAD:]
        return _silu(y)

    def l2n(y, scale=1.0):
        return y * (lax.rsqrt(jnp.sum(y * y, axis=-1, keepdims=True) + EPS) * scale)

    for h in range(H):
        cols = slice(h * E, (h + 1) * E)
        q = l2n(conv_silu(h), E ** -0.5)
        k = l2n(conv_silu(H + h))
        v = conv_silu(2 * H + h)
        beta = cols_t[:, H + h:H + h + 1]
        eg = cols_t[:, 2 * H + h:2 * H + h + 1]
        kb = k * beta
        k_ref[0, :, cols] = k.astype(BF16)
        kb_ref[0, :, cols] = kb.astype(BF16)
        vb_ref[0, :, cols] = (v * beta).astype(BF16)
        kbg_ref[0, :, cols] = (kb * eg).astype(BF16)
        q_ref[0, :, cols] = q.astype(BF16)
        qd_ref[0, :, cols] = (q * eg).astype(BF16)
        kd_ref[0, :, cols] = (k * cols_t[:, 3 * H + h:3 * H + h + 1]).astype(BF16)

    y_scr[0:PAD] = y_scr[tm:tm + PAD]


def _dn_proj(x, gain, w_qkv, w_z, w_abt, conv_w, alog_c, dtb_c):
    B, S, D = x.shape
    tm = min(S, 512)
    W = DN_WIDTH
    big = jax.ShapeDtypeStruct((B, S, W), BF16)
    bspec = pl.BlockSpec((1, tm, W), lambda b, s: (b, s, 0))
    const = lambda shape: pl.BlockSpec(shape, lambda b, s: (0,) * len(shape))
    return pl.pallas_call(
        functools.partial(_dn_proj_kernel, tm=tm),
        grid=(B, S // tm),
        in_specs=[pl.BlockSpec((1, tm, D), lambda b, s: (b, s, 0)), const((1, D)),
                  const((D, 3 * W)), const((D, W)), const((2 * DN_HEADS, D)),
                  const((CONV_WIDTH, 3 * W)), const((DN_HEADS, 1)), const((DN_HEADS, 1))],
        out_specs=[bspec] * 8 + [pl.BlockSpec((1, tm, V7X_LANES), lambda b, s: (b, s, 0)),
                                 pl.BlockSpec((1, DN_HEADS, tm), lambda b, s: (b, 0, s))],
        out_shape=[big] * 8 + [jax.ShapeDtypeStruct((B, S, V7X_LANES), F32),
                               jax.ShapeDtypeStruct((B, DN_HEADS, S), F32)],
        scratch_shapes=[pltpu.VMEM((tm + V7X_SUBLANES, 3 * W), F32)],
        compiler_params=_params(("parallel", "arbitrary")),
        name="deltanet_proj",
    )(x, gain, w_qkv, w_z, w_abt, conv_w, alog_c, dtb_c)


def _delta_kernel(k_ref, kb_ref, vb_ref, kbg_ref, q_ref, qd_ref, kd_ref, z_ref, gcol_ref, grow_ref, og_ref,
                  out_ref, state, *, tc, nh):
    hb = pl.program_id(1)
    n = pl.program_id(2)
    C, E = CHUNK, DN_HEAD_DIM
    nc = tc // C

    @pl.when(n == 0)
    def _():
        state[...] = jnp.zeros_like(state)

    lane = lax.broadcasted_iota(jnp.int32, (tc, V7X_LANES), 1)
    ri = lax.broadcasted_iota(jnp.int32, (C, C), 0)
    ci = lax.broadcasted_iota(jnp.int32, (C, C), 1)
    lower = ri >= ci
    strict = ri > ci
    gam_c, gam_r = [], []
    for i in range(nh):
        h = hb * nh + i
        gam_c.append(jnp.sum(jnp.where(lane == h, gcol_ref[0], 0.0), axis=-1, keepdims=True))
        gam_r.append(grow_ref[0, pl.ds(h, 1), :])

    units = [(i, c) for i in range(nh) for c in range(nc)]
    blk = lambda ref, i, c: ref[0, c * C:(c + 1) * C, i * E:(i + 1) * E]

    attn, pw = [], []
    for i, c in units:
        rows = slice(c * C, (c + 1) * C)
        diff = gam_c[i][rows] - gam_r[i][:, rows]
        decay = jnp.exp(jnp.where(lower, diff, NEG_BIG))
        gram = _dot_nt(jnp.concatenate([blk(q_ref, i, c), blk(kb_ref, i, c)], axis=0), blk(k_ref, i, c))
        attn.append((gram[:C] * decay).astype(BF16))
        pw.append(-jnp.where(strict, gram[C:] * decay, 0.0))
    nm = list(pw)
    for _ in range(int(math.log2(C)) - 1):
        pwb = [x.astype(BF16) for x in pw]
        pw = [_dot(x, x) for x in pwb]
        pwb = [x.astype(BF16) for x in pw]
        nm = [a + b + _dot(a.astype(BF16), bb) for a, b, bb in zip(nm, pw, pwb)]
    u, w = [], []
    for (i, c), a in zip(units, nm):
        rhs = jnp.concatenate([blk(vb_ref, i, c), blk(kbg_ref, i, c)], axis=1)
        sol = rhs.astype(F32) + _dot(a.astype(BF16), rhs)
        u.append(sol[:, :E])
        w.append(sol[:, E:].astype(BF16))

    S = [state[i] for i in range(nh)]
    for c in range(nc):
        rows = slice(c * C, (c + 1) * C)
        ws = [_dot(jnp.concatenate([w[i * nc + c], blk(qd_ref, i, c)], axis=0), S[i].astype(BF16))
              for i in range(nh)]
        vnb = [(u[i * nc + c] - ws[i][:C]).astype(BF16) for i in range(nh)]
        o = [ws[i][C:] + _dot(attn[i * nc + c], vnb[i]) for i in range(nh)]
        S = [S[i] * jnp.exp(gam_c[i][c * C + C - 1:c * C + C, :]) + _dot_tn(blk(kd_ref, i, c), vnb[i])
             for i in range(nh)]
        for i in range(nh):
            zc = blk(z_ref, i, c).astype(F32)
            out_ref[0, rows, i * E:(i + 1) * E] = (
                _rms(o[i], og_ref[...]) * _silu(zc)).astype(BF16)
    for i in range(nh):
        state[i] = S[i]


def _delta_rule(k, kb, vb, kbg, q, qd, kd, z, gcol, grow, o_gain):
    B, S, W = k.shape
    E = DN_HEAD_DIM
    tc = min(S, 256)
    nh = DN_HEADS
    hspec = pl.BlockSpec((1, tc, nh * E), lambda b, h, n: (b, n, h))
    return pl.pallas_call(
        functools.partial(_delta_kernel, tc=tc, nh=nh),
        grid=(B, DN_HEADS // nh, S // tc),
        in_specs=[hspec] * 8 + [pl.BlockSpec((1, tc, V7X_LANES), lambda b, h, n: (b, n, 0)),
                                pl.BlockSpec((1, DN_HEADS, tc), lambda b, h, n: (b, 0, n)),
                                pl.BlockSpec((1, E), lambda b, h, n: (0, 0))],
        out_specs=hspec,
        out_shape=jax.ShapeDtypeStruct((B, S, W), BF16),
        scratch_shapes=[pltpu.VMEM((nh, E, E), F32)],
        compiler_params=_params(("parallel", "parallel", "arbitrary")),
        name="delta_rule",
    )(k, kb, vb, kbg, q, qd, kd, z, gcol, grow, o_gain)


def _attention_layer(x, tables, mix_gain, w_qkv, q_gain, k_gain, w_o):
    B, S, D = x.shape
    gain = mix_gain[None, :]
    heads_per_pass = V7X_MXU_DIM // A_HEAD_DIM
    bd = jnp.kron(jnp.eye(heads_per_pass, dtype=F32),
                  jnp.full((A_HEAD_DIM, A_HEAD_DIM), 1.0 / A_HEAD_DIM, F32)).astype(BF16)
    q_gain = q_gain * (A_HEAD_DIM ** -0.5)
    assert all(window // dilation == BAND for window, dilation in SWA_GROUPS)
    qkv = _qkv_all(x, gain, w_qkv.astype(BF16), jnp.tile(q_gain, (1, A_HEADS)), jnp.tile(k_gain, (1, A_HEADS)),
                   tables, bd)
    os, ls = [], []
    for (q, k, v), (_, dilation) in zip(qkv, SWA_GROUPS):
        o, lse = _attn_group(q, k, v, dilation)
        os.append(o)
        ls.append(lse)
    return _merge_out(os, ls, x, w_o.astype(BF16))


def _deltanet_layer(x, mix_gain, w_in, conv_w, a_log, dt_bias, o_gain):
    B, S, D = x.shape
    W, H = DN_WIDTH, DN_HEADS
    c = 3 * W
    w_qkv = w_in[:, :c].astype(BF16)
    w_a = w_in[:, c:c + H]
    w_b = w_in[:, c + H:c + 2 * H]
    w_z = w_in[:, c + 2 * H:].astype(BF16)
    w_abt = jnp.concatenate([w_a, w_b], axis=1).T.astype(BF16)
    outs = _dn_proj(x, mix_gain[None, :], w_qkv, w_z, w_abt, conv_w, a_log[:, None], dt_bias[:, None])
    return _delta_rule(*outs, o_gain[None, :]).reshape(B * S, W)


def kernel(x, p, positions, mix_norm, attn_w_qkv, attn_q_gain, attn_k_gain, attn_w_o, dn_w_in, dn_conv,
           dn_a_log, dn_dt_bias, dn_o_gain, dn_w_o, mlp_norm, w_up, w_down, ple_norm, w_ple, w_ple_gate):
    B, S, D = x.shape
    depth = p.shape[0]
    tables = _rope_tables(positions)
    tables = [t.reshape(B, S, V7X_LANES) for t in tables]
    for i in range(depth):
        j = i // 2
        mix, w_mix = None, None
        if i % 2 == 0:
            x = _attention_layer(x, tables, mix_norm[i], attn_w_qkv[j], attn_q_gain[j], attn_k_gain[j],
                                 attn_w_o[j])
        else:
            mix = _deltanet_layer(x, mix_norm[i], dn_w_in[j], dn_conv[j], dn_a_log[j], dn_dt_bias[j],
                                  dn_o_gain[j])
            w_mix = dn_w_o[j].astype(BF16)
        x = _mlp(x.reshape(B * S, D), mlp_norm[i][None, :], w_up[i].astype(BF16), w_down[i].astype(BF16),
                 p[i].reshape(B * S, -1), ple_norm[i][None, :], w_ple[i].astype(BF16),
                 w_ple_gate[i].astype(BF16), mix=mix, w_mix=w_mix).reshape(B, S, D)
    return x
```
